```python
import math
import jax, jax.numpy as jnp
from jax import lax
import numpy as np

D_MODEL = 1024
BATCH = 8
SEQ = 4096
DEPTH = 2

N_A = DEPTH // 2
N_B = DEPTH - N_A
HEAD_DIM = 64
ROPE_DIM = HEAD_DIM // 4
ROPE_THETA = 500000.0
RMS_EPS = 1e-6
NEG_INF = -1e30
MEM_HEADS = 4
MEM_WIDTH = MEM_HEADS * HEAD_DIM
MIX_WIDTH = D_MODEL
LRU_WIDTH = MIX_WIDTH - MEM_WIDTH
LRU_BLOCKS = LRU_WIDTH // HEAD_DIM
CONV_WIDTH = 4
LRU_C = 8.0
NSA_WIDTH = MIX_WIDTH - MEM_WIDTH
NSA_HEADS = NSA_WIDTH // HEAD_DIM
NSA_KV_HEADS = 2
N_BRANCH = 3
CMP_LEN = 32
CMP_STRIDE = 16
CMP_HIDDEN = 256
SEL_LEN = 64
SEL_TOP = 16
SEL_FORCE_SCORE = 1e4
WINDOW = 512
Q_CHUNK = 64
N_KEYS = 128
N_EXPERTS = N_KEYS * N_KEYS
PEER_HEADS = 8
PEER_KEY_DIM = 256
PEER_TOPK = 16
PEER_CHUNK_MAX = 512

kernel_name = "hybrid_rglru_nsa_peer_yoco"


def rmsnorm(x, g):
    xf = x.astype(jnp.float32)
    y = xf * lax.rsqrt(jnp.mean(xf * xf, axis=-1, keepdims=True) + RMS_EPS)
    return (y * g.astype(jnp.float32)).astype(x.dtype)


def rope(x, pos):
    half = ROPE_DIM // 2
    freqs = ROPE_THETA ** (-jnp.arange(half, dtype=jnp.float32) / half)
    ang = pos.astype(jnp.float32)[:, None] * freqs[None, :]
    cos = jnp.cos(ang)[:, None, :]
    sin = jnp.sin(ang)[:, None, :]
    xf = x.astype(jnp.float32)
    x1, x2, rest = xf[..., :half], xf[..., half:ROPE_DIM], xf[..., ROPE_DIM:]
    out = jnp.concatenate([x1 * cos - x2 * sin, x2 * cos + x1 * sin, rest], axis=-1)
    return out.astype(x.dtype)


def masked_softmax(s, mask):
    s = jnp.where(mask, s.astype(jnp.float32), NEG_INF)
    p = jax.nn.softmax(s, axis=-1)
    return jnp.where(mask, p, 0.0)


def causal_conv(x, w, b):
    K, C = w.shape
    y = lax.conv_general_dilated(x, w[:, None, :].astype(x.dtype), window_strides=(1,),
                                 padding=[(K - 1, 0)], dimension_numbers=('NWC', 'WIO', 'NWC'),
                                 feature_group_count=C)
    return y + b.astype(x.dtype)


def rg_lru(x, gate_w, gate_b, lam):
    B, S, C = x.shape
    xb = x.reshape(B, S, LRU_BLOCKS, HEAD_DIM)
    gates = jnp.einsum('bsnd,gnde->gbsne', xb, gate_w).reshape(2, B, S, C).astype(jnp.float32)
    gates = gates + gate_b.astype(jnp.float32)[:, None, None, :]
    r = jax.nn.sigmoid(gates[0])
    i = jax.nn.sigmoid(gates[1])
    log_a = -LRU_C * r * jax.nn.softplus(-lam.astype(jnp.float32))
    a = jnp.exp(log_a)
    u = jnp.sqrt(-jnp.expm1(2.0 * log_a)) * (i * x.astype(jnp.float32))

    def combine(left, right):
        a1, b1 = left
        a2, b2 = right
        return a1 * a2, a2 * b1 + b2

    _, h = lax.associative_scan(combine, (a, u), axis=1)
    return h.astype(x.dtype)


def memory_attention(qm, mem, norm_g, w_kv, q_gain, k_gain):
    B, S, _ = qm.shape
    M = mem.shape[1]
    kv = (rmsnorm(mem, norm_g) @ w_kv).reshape(B, M, 2, MEM_HEADS, HEAD_DIM)
    q = rmsnorm(qm.reshape(B, S, MEM_HEADS, HEAD_DIM), q_gain)
    k = rmsnorm(kv[:, :, 0], k_gain)
    v = kv[:, :, 1]
    s = jnp.einsum('bshd,bmhd->bhsm', q, k) * (HEAD_DIM ** -0.5)
    p = jax.nn.softmax(s.astype(jnp.float32), axis=-1).astype(v.dtype)
    return jnp.einsum('bhsm,bmhd->bshd', p, v).reshape(B, S, MEM_WIDTH)


def nsa_shared_kv(x, positions, kv_norm, w_kv_shared, k_gain_shared, cmp_pos, cmp_w1, cmp_b1, cmp_w2):
    B, S, _ = x.shape
    G, dh = NSA_KV_HEADS, HEAD_DIM
    kv = (rmsnorm(x, kv_norm) @ w_kv_shared).reshape(B, S, N_BRANCH, 2, G, dh)
    n_cmp = (S - CMP_LEN) // CMP_STRIDE + 1
    idx = np.arange(n_cmp)[:, None] * CMP_STRIDE + np.arange(CMP_LEN)[None, :]

    def compress(tok, j):
        blk = tok[:, idx] + cmp_pos[j][None, None, :, None, :]
        blk = blk.transpose(0, 1, 3, 2, 4).reshape(B, n_cmp, G, CMP_LEN * dh)
        hid = jax.nn.gelu(blk @ cmp_w1[j] + cmp_b1[j])
        return hid @ cmp_w2[j]

    kc = rope(rmsnorm(compress(kv[:, :, 0, 0], 0), k_gain_shared[0]), positions[idx[:, -1]])
    vc = compress(kv[:, :, 0, 1], 1)
    ks = rope(rmsnorm(kv[:, :, 1, 0], k_gain_shared[1]), positions)
    vs = kv[:, :, 1, 1]
    kw = rope(rmsnorm(kv[:, :, 2, 0], k_gain_shared[2]), positions)
    vw = kv[:, :, 2, 1]
    t = lambda a: a.transpose(0, 2, 1, 3)
    return (t(kc), t(vc), t(ks), t(vs), t(kw), t(vw))


def _cmp_sel_overlap(n_cmp, n_sel):
    c0 = np.arange(n_cmp)[:, None] * CMP_STRIDE
    s0 = np.arange(n_sel)[None, :] * SEL_LEN
    ov = np.minimum(c0 + CMP_LEN, s0 + SEL_LEN) - np.maximum(c0, s0)
    return (np.clip(ov, 0, None) / CMP_LEN).astype(np.float32)


def nsa_attention(q, kc, vc, ks, vs, kw, vw):
    B, S, H, dh = q.shape
    G = kc.shape[1]
    R = H // G
    C = Q_CHUNK
    NQ = S // C
    n_cmp = kc.shape[2]
    n_sel = S // SEL_LEN
    top = min(SEL_TOP, n_sel)
    scale = dh ** -0.5
    cmp_end = jnp.arange(n_cmp) * CMP_STRIDE + CMP_LEN - 1
    overlap = jnp.asarray(_cmp_sel_overlap(n_cmp, n_sel))
    ks_blk = ks.reshape(B, G, n_sel, SEL_LEN, dh)
    vs_blk = vs.reshape(B, G, n_sel, SEL_LEN, dh)
    kw_pad = jnp.pad(kw, ((0, 0), (0, 0), (WINDOW, 0), (0, 0)))
    vw_pad = jnp.pad(vw, ((0, 0), (0, 0), (WINDOW, 0), (0, 0)))
    q_chunks = q.reshape(B, NQ, C, G, R, dh).transpose(1, 0, 3, 4, 2, 5)
    b_ix = jnp.arange(B)[:, None, None, None]
    g_ix = jnp.arange(G)[None, :, None, None]
    blk_ids = jnp.arange(n_sel)

    def chunk(args):
        qc, c = args
        t = c * C + jnp.arange(C)
        s = jnp.einsum('bgrqd,bgnd->bgrqn', qc, kc) * scale
        p_cmp = masked_softmax(s, cmp_end[None, :] <= t[:, None]).astype(vc.dtype)
        o_cmp = jnp.einsum('bgrqn,bgnd->bgrqd', p_cmp, vc)
        imp = jnp.einsum('bgrqn,nj->bgqj', p_cmp, overlap.astype(p_cmp.dtype)).astype(jnp.float32)
        cur = t // SEL_LEN
        valid = blk_ids[None, :] <= cur[:, None]
        forced = (blk_ids[None, :] == 0) | (blk_ids[None, :] == cur[:, None]) | (blk_ids[None, :] == cur[:, None] - 1)
        imp = jnp.where(forced, SEL_FORCE_SCORE, imp)
        imp = jnp.where(valid, imp, -1.0)
        vals, idx = lax.top_k(imp, top)
        kg = ks_blk[b_ix, g_ix, idx]
        vg = vs_blk[b_ix, g_ix, idx].reshape(B, G, C, top * SEL_LEN, dh)
        s = jnp.einsum('bgrqd,bgqkld->bgrqkl', qc, kg) * scale
        kpos = idx[..., None] * SEL_LEN + jnp.arange(SEL_LEN)
        m = (vals >= 0.0)[..., None] & (kpos <= t[None, None, :, None, None])
        p_sel = masked_softmax(s.reshape(B, G, R, C, top * SEL_LEN),
                               m.reshape(B, G, 1, C, top * SEL_LEN)).astype(vg.dtype)
        o_sel = jnp.einsum('bgrqk,bgqkd->bgrqd', p_sel, vg)
        kwin = lax.dynamic_slice_in_dim(kw_pad, c * C, WINDOW + C, axis=2)
        vwin = lax.dynamic_slice_in_dim(vw_pad, c * C, WINDOW + C, axis=2)
        wpos = c * C - WINDOW + jnp.arange(WINDOW + C)
        m = (wpos[None, :] <= t[:, None]) & (wpos[None, :] > t[:, None] - WINDOW) & (wpos[None, :] >= 0)
        s = jnp.einsum('bgrqd,bgkd->bgrqk', qc, kwin) * scale
        p_win = masked_softmax(s, m).astype(vwin.dtype)
        o_win = jnp.einsum('bgrqk,bgkd->bgrqd', p_win, vwin)
        return (o_cmp, o_sel, o_win)

    outs = lax.map(chunk, (q_chunks, jnp.arange(NQ)))
    back = lambda o: o.transpose(1, 0, 4, 2, 3, 5).reshape(B, S, H, dh)
    return (back(outs[0]), back(outs[1]), back(outs[2]))


def peer(h, w_q, sub_keys, u_tab, v_tab):
    B, S, D = h.shape
    T = B * S
    ch = math.gcd(T, PEER_CHUNK_MAX)
    half = PEER_KEY_DIM // 2

    def chunk(xc):
        q = (xc @ w_q).reshape(ch, PEER_HEADS, 2, half)
        s = jnp.einsum('thpd,hpnd->thpn', q, sub_keys).astype(jnp.float32)
        v1, i1 = lax.top_k(s[:, :, 0], PEER_TOPK)
        v2, i2 = lax.top_k(s[:, :, 1], PEER_TOPK)
        cand = (v1[..., :, None] + v2[..., None, :]).reshape(ch, PEER_HEADS, PEER_TOPK * PEER_TOPK)
        vals, ci = lax.top_k(cand, PEER_TOPK)
        e = (jnp.take_along_axis(i1, ci // PEER_TOPK, axis=-1) * N_KEYS
             + jnp.take_along_axis(i2, ci % PEER_TOPK, axis=-1))
        g = jax.nn.softmax(vals, axis=-1).astype(xc.dtype)
        u = u_tab[e]
        act = jax.nn.gelu(jnp.einsum('td,thkd->thk', xc, u)) * g
        return jnp.einsum('thk,thkd->td', act, v_tab[e])

    return lax.map(chunk, h.reshape(T // ch, ch, D)).reshape(B, S, D)


def setup_inputs(seed: int = 0) -> dict:
    key = jax.random.key(seed)
    counter = [0]

    def nk():
        counter[0] += 1
        return jax.random.fold_in(key, counter[0])

    def nrm(shape, scale):
        return scale * jax.random.normal(nk(), shape, jnp.float32)

    def gain(shape):
        return 1.0 + 0.02 * jax.random.normal(nk(), shape, jnp.float32)

    D = D_MODEL
    u = jax.random.uniform(nk(), (N_A, LRU_WIDTH), jnp.float32, minval=0.9, maxval=0.999)
    a0 = u ** (1.0 / LRU_C)
    a_lambda = jnp.log(a0) - jnp.log1p(-a0)
    return {
        "x": nrm((BATCH, SEQ, D), 1.0),
        "mem": nrm((BATCH, 256, D), 1.0),
        "positions": jnp.arange(SEQ, dtype=jnp.int32),
        "norm_mix": gain((DEPTH, D)),
        "norm_ffn": gain((DEPTH, D)),
        "norm_mem": gain((DEPTH, D)),
        "w_out": nrm((DEPTH, MIX_WIDTH, D), MIX_WIDTH ** -0.5),
        "w_mem_kv": nrm((DEPTH, D, 2 * MEM_WIDTH), D ** -0.5),
        "mem_q_gain": gain((DEPTH, HEAD_DIM)),
        "mem_k_gain": gain((DEPTH, HEAD_DIM)),
        "a_w_in": nrm((N_A, D, 2 * LRU_WIDTH + MEM_WIDTH), D ** -0.5),
        "a_conv_w": nrm((N_A, CONV_WIDTH, LRU_WIDTH), CONV_WIDTH ** -0.5),
        "a_conv_b": nrm((N_A, LRU_WIDTH), 0.02),
        "a_gate_w": nrm((N_A, 2, LRU_BLOCKS, HEAD_DIM, HEAD_DIM), HEAD_DIM ** -0.5),
        "a_gate_b": nrm((N_A, 2, LRU_WIDTH), 0.02),
        "a_lambda": a_lambda,
        "b_w_in": nrm((N_B, D, NSA_WIDTH + N_BRANCH * NSA_HEADS + MEM_WIDTH), D ** -0.5),
        "b_gate_b": nrm((N_B, N_BRANCH * NSA_HEADS), 0.02),
        "b_q_gain": gain((N_B, HEAD_DIM)),
        "kv_norm": gain((D,)),
        "w_kv_shared": nrm((D, N_BRANCH * 2 * NSA_KV_HEADS * HEAD_DIM), D ** -0.5),
        "k_gain_shared": gain((N_BRANCH, HEAD_DIM)),
        "cmp_pos": nrm((2, CMP_LEN, HEAD_DIM), 0.1),
        "cmp_w1": nrm((2, CMP_LEN * HEAD_DIM, CMP_HIDDEN), (CMP_LEN * HEAD_DIM) ** -0.5),
        "cmp_b1": nrm((2, CMP_HIDDEN), 0.02),
        "cmp_w2": nrm((2, CMP_HIDDEN, HEAD_DIM), CMP_HIDDEN ** -0.5),
        "peer_wq": nrm((DEPTH, D, PEER_HEADS * PEER_KEY_DIM), D ** -0.5),
        "peer_subkeys": nrm((DEPTH, PEER_HEADS, 2, N_KEYS, PEER_KEY_DIM // 2), (PEER_KEY_DIM // 2) ** -0.5),
        "peer_u": nrm((DEPTH, N_EXPERTS, D), D ** -0.5),
        "peer_v": nrm((DEPTH, N_EXPERTS, D), PEER_HEADS ** -0.5),
    }


def reference(x, mem, positions, norm_mix, norm_ffn, norm_mem, w_out, w_mem_kv, mem_q_gain, mem_k_gain,
              a_w_in, a_conv_w, a_conv_b, a_gate_w, a_gate_b, a_lambda, b_w_in, b_gate_b, b_q_gain,
              kv_norm, w_kv_shared, k_gain_shared, cmp_pos, cmp_w1, cmp_b1, cmp_w2,
              peer_wq, peer_subkeys, peer_u, peer_v):
    B, S, _ = x.shape
    shared = None
    for l in range(DEPTH):
        h = rmsnorm(x, norm_mix[l])
        if l < N_A:
            proj = h @ a_w_in[l]
            xb = causal_conv(proj[..., :LRU_WIDTH], a_conv_w[l], a_conv_b[l])
            yb = proj[..., LRU_WIDTH:2 * LRU_WIDTH]
            qm = proj[..., 2 * LRU_WIDTH:]
            mix = rg_lru(xb, a_gate_w[l], a_gate_b[l], a_lambda[l]) * jax.nn.gelu(yb)
        else:
            j = l - N_A
            proj = h @ b_w_in[j]
            q = proj[..., :NSA_WIDTH].reshape(B, S, NSA_HEADS, HEAD_DIM)
            gl = (proj[..., NSA_WIDTH:NSA_WIDTH + N_BRANCH * NSA_HEADS].reshape(B, S, NSA_HEADS, N_BRANCH)
                  + b_gate_b[j].reshape(NSA_HEADS, N_BRANCH))
            qm = proj[..., NSA_WIDTH + N_BRANCH * NSA_HEADS:]
            q = rope(rmsnorm(q, b_q_gain[j]), positions)
            o_cmp, o_sel, o_win = nsa_attention(q, shared[0], shared[1], shared[2], shared[3], shared[4], shared[5])
            g = jax.nn.sigmoid(gl.astype(jnp.float32)).astype(q.dtype)
            mix = (g[..., 0:1] * o_cmp + g[..., 1:2] * o_sel + g[..., 2:3] * o_win).reshape(B, S, NSA_WIDTH)
        mo = memory_attention(qm, mem, norm_mem[l], w_mem_kv[l], mem_q_gain[l], mem_k_gain[l])
        x = x + jnp.concatenate([mix, mo], axis=-1) @ w_out[l]
        x = x + peer(rmsnorm(x, norm_ffn[l]), peer_wq[l], peer_subkeys[l], peer_u[l], peer_v[l])
        if l == N_A - 1:
            shared = nsa_shared_kv(x, positions, kv_norm, w_kv_shared, k_gain_shared,
                                   cmp_pos, cmp_w1, cmp_b1, cmp_w2)
    return x
```

```python
import functools
import math

import jax
import jax.numpy as jnp
import numpy as np
from jax import lax
from jax.experimental import pallas as pl
from jax.experimental.pallas import tpu as pltpu

D_MODEL = 1024
DEPTH = 2
N_A = DEPTH // 2
HEAD_DIM = 64
ROPE_DIM = HEAD_DIM // 4
ROPE_THETA = 500000.0
RMS_EPS = 1e-6
NEG_INF = -1e30
MEM_HEADS = 4
MEM_WIDTH = MEM_HEADS * HEAD_DIM
MIX_WIDTH = D_MODEL
LRU_WIDTH = MIX_WIDTH - MEM_WIDTH
LRU_BLOCKS = LRU_WIDTH // HEAD_DIM
CONV_WIDTH = 4
LRU_C = 8.0
NSA_WIDTH = MIX_WIDTH - MEM_WIDTH
NSA_HEADS = NSA_WIDTH // HEAD_DIM
NSA_KV_HEADS = 2
N_BRANCH = 3
CMP_LEN = 32
CMP_STRIDE = 16
SEL_LEN = 64
SEL_TOP = 16
SEL_FORCE_SCORE = 1e4
WINDOW = 512
Q_CHUNK = 64
N_KEYS = 128
PEER_HEADS = 8
PEER_KEY_DIM = 256
PEER_TOPK = 16
PEER_CHUNK_MAX = 512

VMEM_LIMIT_BYTES = 48 * 1024 * 1024
ROW_TILE = 512


def _norm_matmul_kernel(x_ref, g_ref, w_ref, o_ref):
    x = x_ref[...]
    ms = jnp.mean(x * x, axis=-1, keepdims=True)
    y = x * lax.rsqrt(ms + RMS_EPS) * g_ref[...]
    o_ref[...] = jnp.dot(y.astype(jnp.bfloat16), w_ref[...],
                         preferred_element_type=jnp.float32)


def norm_matmul(x2d, g, w):
    T, D = x2d.shape
    N = w.shape[1]
    return pl.pallas_call(
        _norm_matmul_kernel,
        grid=(T // ROW_TILE,),
        in_specs=[
            pl.BlockSpec((ROW_TILE, D), lambda i: (i, 0)),
            pl.BlockSpec((1, D), lambda i: (0, 0)),
            pl.BlockSpec((D, N), lambda i: (0, 0)),
        ],
        out_specs=pl.BlockSpec((ROW_TILE, N), lambda i: (i, 0)),
        out_shape=jax.ShapeDtypeStruct((T, N), jnp.float32),
        compiler_params=pltpu.CompilerParams(
            dimension_semantics=("arbitrary",), vmem_limit_bytes=VMEM_LIMIT_BYTES),
        name="norm_matmul",
    )(x2d, g.reshape(1, D), w.astype(jnp.bfloat16))


def _matmul_residual_kernel(a_ref, w_ref, r_ref, o_ref):
    o_ref[...] = r_ref[...] + jnp.dot(a_ref[...].astype(jnp.bfloat16), w_ref[...],
                                      preferred_element_type=jnp.float32)


def matmul_residual(a2d, w, res2d):
    T, K = a2d.shape
    N = w.shape[1]
    return pl.pallas_call(
        _matmul_residual_kernel,
        grid=(T // ROW_TILE,),
        in_specs=[
            pl.BlockSpec((ROW_TILE, K), lambda i: (i, 0)),
            pl.BlockSpec((K, N), lambda i: (0, 0)),
            pl.BlockSpec((ROW_TILE, N), lambda i: (i, 0)),
        ],
        out_specs=pl.BlockSpec((ROW_TILE, N), lambda i: (i, 0)),
        out_shape=jax.ShapeDtypeStruct((T, N), jnp.float32),
        compiler_params=pltpu.CompilerParams(
            dimension_semantics=("arbitrary",), vmem_limit_bytes=VMEM_LIMIT_BYTES),
        name="matmul_residual",
    )(a2d, w.astype(jnp.bfloat16), res2d)


def _rmsnorm(x, g):
    y = x * lax.rsqrt(jnp.mean(x * x, axis=-1, keepdims=True) + RMS_EPS)
    return y * g


def _rope(x, pos):
    half = ROPE_DIM // 2
    freqs = ROPE_THETA ** (-jnp.arange(half, dtype=jnp.float32) / half)
    ang = pos.astype(jnp.float32)[:, None] * freqs[None, :]
    cos = jnp.cos(ang)[:, None, :]
    sin = jnp.sin(ang)[:, None, :]
    x1, x2, rest = x[..., :half], x[..., half:ROPE_DIM], x[..., ROPE_DIM:]
    return jnp.concatenate([x1 * cos - x2 * sin, x2 * cos + x1 * sin, rest], axis=-1)


def _masked_softmax(s, mask):
    s = jnp.where(mask, s, NEG_INF)
    p = jax.nn.softmax(s, axis=-1)
    return jnp.where(mask, p, 0.0)


def _causal_conv(x, w, b):
    K, C = w.shape
    y = lax.conv_general_dilated(x, w[:, None, :], window_strides=(1,),
                                 padding=[(K - 1, 0)], dimension_numbers=('NWC', 'WIO', 'NWC'),
                                 feature_group_count=C)
    return y + b


def _rg_lru(x, gate_w, gate_b, lam):
    B, S, C = x.shape
    xb = x.reshape(B, S, LRU_BLOCKS, HEAD_DIM)
    gates = jnp.einsum('bsnd,gnde->gbsne', xb, gate_w).reshape(2, B, S, C)
    gates = gates + gate_b[:, None, None, :]
    r = jax.nn.sigmoid(gates[0])
    i = jax.nn.sigmoid(gates[1])
    log_a = -LRU_C * r * jax.nn.softplus(-lam)
    a = jnp.exp(log_a)
    u = jnp.sqrt(-jnp.expm1(2.0 * log_a)) * (i * x)

    def combine(left, right):
        a1, b1 = left
        a2, b2 = right
        return a1 * a2, a2 * b1 + b2

    _, h = lax.associative_scan(combine, (a, u), axis=1)
    return h


def _memory_attention(qm, mem, norm_g, w_kv, q_gain, k_gain):
    B, S, _ = qm.shape
    M = mem.shape[1]
    kv = norm_matmul(mem.reshape(B * M, D_MODEL), norm_g, w_kv).reshape(B, M, 2, MEM_HEADS, HEAD_DIM)
    q = _rmsnorm(qm.reshape(B, S, MEM_HEADS, HEAD_DIM), q_gain)
    k = _rmsnorm(kv[:, :, 0], k_gain)
    v = kv[:, :, 1]
    s = jnp.einsum('bshd,bmhd->bhsm', q, k) * (HEAD_DIM ** -0.5)
    p = jax.nn.softmax(s, axis=-1)
    return jnp.einsum('bhsm,bmhd->bshd', p, v).reshape(B, S, MEM_WIDTH)


def _nsa_shared_kv(x, positions, kv_norm, w_kv_shared, k_gain_shared, cmp_pos, cmp_w1, cmp_b1, cmp_w2):
    B, S, _ = x.shape
    G, dh = NSA_KV_HEADS, HEAD_DIM
    kv = norm_matmul(x.reshape(B * S, D_MODEL), kv_norm, w_kv_shared).reshape(B, S, N_BRANCH, 2, G, dh)
    n_cmp = (S - CMP_LEN) // CMP_STRIDE + 1
    idx = np.arange(n_cmp)[:, None] * CMP_STRIDE + np.arange(CMP_LEN)[None, :]

    def compress(tok, j):
        blk = tok[:, idx] + cmp_pos[j][None, None, :, None, :]
        blk = blk.transpose(0, 1, 3, 2, 4).reshape(B, n_cmp, G, CMP_LEN * dh)
        hid = jax.nn.gelu(blk @ cmp_w1[j] + cmp_b1[j])
        return hid @ cmp_w2[j]

    kc = _rope(_rmsnorm(compress(kv[:, :, 0, 0], 0), k_gain_shared[0]), positions[idx[:, -1]])
    vc = compress(kv[:, :, 0, 1], 1)
    ks = _rope(_rmsnorm(kv[:, :, 1, 0], k_gain_shared[1]), positions)
    vs = kv[:, :, 1, 1]
    kw = _rope(_rmsnorm(kv[:, :, 2, 0], k_gain_shared[2]), positions)
    vw = kv[:, :, 2, 1]
    t = lambda a: a.transpose(0, 2, 1, 3)
    return (t(kc), t(vc), t(ks), t(vs), t(kw), t(vw))


def _cmp_sel_overlap(n_cmp, n_sel):
    c0 = np.arange(n_cmp)[:, None] * CMP_STRIDE
    s0 = np.arange(n_sel)[None, :] * SEL_LEN
    ov = np.minimum(c0 + CMP_LEN, s0 + SEL_LEN) - np.maximum(c0, s0)
    return (np.clip(ov, 0, None) / CMP_LEN).astype(np.float32)


def _nsa_attention(q, kc, vc, ks, vs, kw, vw):
    B, S, H, dh = q.shape
    G = kc.shape[1]
    R = H // G
    C = Q_CHUNK
    NQ = S // C
    n_cmp = kc.shape[2]
    n_sel = S // SEL_LEN
    top = min(SEL_TOP, n_sel)
    scale = dh ** -0.5
    cmp_end = jnp.arange(n_cmp) * CMP_STRIDE + CMP_LEN - 1
    overlap = jnp.asarray(_cmp_sel_overlap(n_cmp, n_sel))
    ks_blk = ks.reshape(B, G, n_sel, SEL_LEN, dh)
    vs_blk = vs.reshape(B, G, n_sel, SEL_LEN, dh)
    kw_pad = jnp.pad(kw, ((0, 0), (0, 0), (WINDOW, 0), (0, 0)))
    vw_pad = jnp.pad(vw, ((0, 0), (0, 0), (WINDOW, 0), (0, 0)))
    q_chunks = q.reshape(B, NQ, C, G, R, dh).transpose(1, 0, 3, 4, 2, 5)
    b_ix = jnp.arange(B)[:, None, None, None]
    g_ix = jnp.arange(G)[None, :, None, None]
    blk_ids = jnp.arange(n_sel)

    def chunk(args):
        qc, c = args
        t = c * C + jnp.arange(C)
        s = jnp.einsum('bgrqd,bgnd->bgrqn', qc, kc) * scale
        p_cmp = _masked_softmax(s, cmp_end[None, :] <= t[:, None])
        o_cmp = jnp.einsum('bgrqn,bgnd->bgrqd', p_cmp, vc)
        imp = jnp.einsum('bgrqn,nj->bgqj', p_cmp, overlap)
        cur = t // SEL_LEN
        valid = blk_ids[None, :] <= cur[:, None]
        forced = (blk_ids[None, :] == 0) | (blk_ids[None, :] == cur[:, None]) | (blk_ids[None, :] == cur[:, None] - 1)
        imp = jnp.where(forced, SEL_FORCE_SCORE, imp)
        imp = jnp.where(valid, imp, -1.0)
        vals, idx = lax.top_k(imp, top)
        kg = ks_blk[b_ix, g_ix, idx]
        vg = vs_blk[b_ix, g_ix, idx].reshape(B, G, C, top * SEL_LEN, dh)
        s = jnp.einsum('bgrqd,bgqkld->bgrqkl', qc, kg) * scale
        kpos = idx[..., None] * SEL_LEN + jnp.arange(SEL_LEN)
        m = (vals >= 0.0)[..., None] & (kpos <= t[None, None, :, None, None])
        p_sel = _masked_softmax(s.reshape(B, G, R, C, top * SEL_LEN),
                                m.reshape(B, G, 1, C, top * SEL_LEN))
        o_sel = jnp.einsum('bgrqk,bgqkd->bgrqd', p_sel, vg)
        kwin = lax.dynamic_slice_in_dim(kw_pad, c * C, WINDOW + C, axis=2)
        vwin = lax.dynamic_slice_in_dim(vw_pad, c * C, WINDOW + C, axis=2)
        wpos = c * C - WINDOW + jnp.arange(WINDOW + C)
        m = (wpos[None, :] <= t[:, None]) & (wpos[None, :] > t[:, None] - WINDOW) & (wpos[None, :] >= 0)
        s = jnp.einsum('bgrqd,bgkd->bgrqk', qc, kwin) * scale
        p_win = _masked_softmax(s, m)
        o_win = jnp.einsum('bgrqk,bgkd->bgrqd', p_win, vwin)
        return (o_cmp, o_sel, o_win)

    outs = lax.map(chunk, (q_chunks, jnp.arange(NQ)))
    back = lambda o: o.transpose(1, 0, 4, 2, 3, 5).reshape(B, S, H, dh)
    return (back(outs[0]), back(outs[1]), back(outs[2]))


def _peer(x2d, norm_g, w_q, sub_keys, u_tab, v_tab):
    T, D = x2d.shape
    ch = math.gcd(T, PEER_CHUNK_MAX)
    half = PEER_KEY_DIM // 2
    q_all = norm_matmul(x2d, norm_g, w_q)

    def chunk(args):
        xr, qc = args
        xc = _rmsnorm(xr, norm_g)
        q = qc.reshape(ch, PEER_HEADS, 2, half)
        s = jnp.einsum('thpd,hpnd->thpn', q, sub_keys)
        v1, i1 = lax.top_k(s[:, :, 0], PEER_TOPK)
        v2, i2 = lax.top_k(s[:, :, 1], PEER_TOPK)
        cand = (v1[..., :, None] + v2[..., None, :]).reshape(ch, PEER_HEADS, PEER_TOPK * PEER_TOPK)
        vals, ci = lax.top_k(cand, PEER_TOPK)
        e = (jnp.take_along_axis(i1, ci // PEER_TOPK, axis=-1) * N_KEYS
             + jnp.take_along_axis(i2, ci % PEER_TOPK, axis=-1))
        g = jax.nn.softmax(vals, axis=-1)
        u = u_tab[e]
        act = jax.nn.gelu(jnp.einsum('td,thkd->thk', xc, u)) * g
        return jnp.einsum('thk,thkd->td', act, v_tab[e])

    out = lax.map(chunk, (x2d.reshape(T // ch, ch, D), q_all.reshape(T // ch, ch, -1)))
    return out.reshape(T, D)


def kernel(x, mem, positions, norm_mix, norm_ffn, norm_mem, w_out, w_mem_kv, mem_q_gain, mem_k_gain, a_w_in, a_conv_w, a_conv_b, a_gate_w, a_gate_b, a_lambda, b_w_in, b_gate_b, b_q_gain, kv_norm, w_kv_shared, k_gain_shared, cmp_pos, cmp_w1, cmp_b1, cmp_w2, peer_wq, peer_subkeys, peer_u, peer_v):
    B, S, D = x.shape
    T = B * S
    x2d = x.reshape(T, D)
    shared = None
    for l in range(DEPTH):
        if l < N_A:
            proj = norm_matmul(x2d, norm_mix[l], a_w_in[l]).reshape(B, S, -1)
            xb = _causal_conv(proj[..., :LRU_WIDTH], a_conv_w[l], a_conv_b[l])
            yb = proj[..., LRU_WIDTH:2 * LRU_WIDTH]
            qm = proj[..., 2 * LRU_WIDTH:]
            mix = _rg_lru(xb, a_gate_w[l], a_gate_b[l], a_lambda[l]) * jax.nn.gelu(yb)
        else:
            j = l - N_A
            proj = norm_matmul(x2d, norm_mix[l], b_w_in[j]).reshape(B, S, -1)
            q = proj[..., :NSA_WIDTH].reshape(B, S, NSA_HEADS, HEAD_DIM)
            gl = (proj[..., NSA_WIDTH:NSA_WIDTH + N_BRANCH * NSA_HEADS].reshape(B, S, NSA_HEADS, N_BRANCH)
                  + b_gate_b[j].reshape(NSA_HEADS, N_BRANCH))
            qm = proj[..., NSA_WIDTH + N_BRANCH * NSA_HEADS:]
            q = _rope(_rmsnorm(q, b_q_gain[j]), positions)
            o_cmp, o_sel, o_win = _nsa_attention(q, *shared)
            g = jax.nn.sigmoid(gl)
            mix = (g[..., 0:1] * o_cmp + g[..., 1:2] * o_sel + g[..., 2:3] * o_win).reshape(B, S, NSA_WIDTH)
        mo = _memory_attention(qm, mem, norm_mem[l], w_mem_kv[l], mem_q_gain[l], mem_k_gain[l])
        cat = jnp.concatenate([mix, mo], axis=-1).reshape(T, MIX_WIDTH)
        x2d = matmul_residual(cat, w_out[l], x2d)
        x2d = x2d + _peer(x2d, norm_ffn[l], peer_wq[l], peer_subkeys[l], peer_u[l], peer_v[l])
        if l == N_A - 1:
            shared = _nsa_shared_kv(x2d.reshape(B, S, D), positions, kv_norm, w_kv_shared, k_gain_shared,
                                    cmp_pos, cmp_w1, cmp_b1, cmp_w2)
    return x2d.reshape(B, S, D)
```

```python
import functools
import math

import jax
import jax.numpy as jnp
import numpy as np
from jax import lax
from jax.experimental import pallas as pl
from jax.experimental.pallas import tpu as pltpu

D_MODEL = 1024
DEPTH = 2
N_A = DEPTH // 2
HEAD_DIM = 64
ROPE_DIM = HEAD_DIM // 4
ROPE_THETA = 500000.0
RMS_EPS = 1e-6
NEG_INF = -1e30
MEM_HEADS = 4
MEM_WIDTH = MEM_HEADS * HEAD_DIM
MIX_WIDTH = D_MODEL
LRU_WIDTH = MIX_WIDTH - MEM_WIDTH
LRU_BLOCKS = LRU_WIDTH // HEAD_DIM
CONV_WIDTH = 4
LRU_C = 8.0
NSA_WIDTH = MIX_WIDTH - MEM_WIDTH
NSA_HEADS = NSA_WIDTH // HEAD_DIM
NSA_KV_HEADS = 2
N_BRANCH = 3
CMP_LEN = 32
CMP_STRIDE = 16
SEL_LEN = 64
SEL_TOP = 16
SEL_FORCE_SCORE = 1e4
WINDOW = 512
Q_CHUNK = 64
N_KEYS = 128
PEER_HEADS = 8
PEER_KEY_DIM = 256
PEER_TOPK = 16
PEER_CHUNK_MAX = 512

VMEM_LIMIT_BYTES = 48 * 1024 * 1024
ROW_TILE = 512


def _norm_matmul_kernel(x_ref, g_ref, w_ref, o_ref):
    x = x_ref[...]
    ms = jnp.mean(x * x, axis=-1, keepdims=True)
    y = x * lax.rsqrt(ms + RMS_EPS) * g_ref[...]
    o_ref[...] = jnp.dot(y.astype(jnp.bfloat16), w_ref[...],
                         preferred_element_type=jnp.float32)


def norm_matmul(x2d, g, w):
    T, D = x2d.shape
    N = w.shape[1]
    return pl.pallas_call(
        _norm_matmul_kernel,
        grid=(T // ROW_TILE,),
        in_specs=[
            pl.BlockSpec((ROW_TILE, D), lambda i: (i, 0)),
            pl.BlockSpec((1, D), lambda i: (0, 0)),
            pl.BlockSpec((D, N), lambda i: (0, 0)),
        ],
        out_specs=pl.BlockSpec((ROW_TILE, N), lambda i: (i, 0)),
        out_shape=jax.ShapeDtypeStruct((T, N), jnp.float32),
        compiler_params=pltpu.CompilerParams(
            dimension_semantics=("arbitrary",), vmem_limit_bytes=VMEM_LIMIT_BYTES),
        name="norm_matmul",
    )(x2d, g.reshape(1, D), w.astype(jnp.bfloat16))


def _matmul_residual_kernel(a_ref, w_ref, r_ref, o_ref):
    o_ref[...] = r_ref[...] + jnp.dot(a_ref[...].astype(jnp.bfloat16), w_ref[...],
                                      preferred_element_type=jnp.float32)


def matmul_residual(a2d, w, res2d):
    T, K = a2d.shape
    N = w.shape[1]
    return pl.pallas_call(
        _matmul_residual_kernel,
        grid=(T // ROW_TILE,),
        in_specs=[
            pl.BlockSpec((ROW_TILE, K), lambda i: (i, 0)),
            pl.BlockSpec((K, N), lambda i: (0, 0)),
            pl.BlockSpec((ROW_TILE, N), lambda i: (i, 0)),
        ],
        out_specs=pl.BlockSpec((ROW_TILE, N), lambda i: (i, 0)),
        out_shape=jax.ShapeDtypeStruct((T, N), jnp.float32),
        compiler_params=pltpu.CompilerParams(
            dimension_semantics=("arbitrary",), vmem_limit_bytes=VMEM_LIMIT_BYTES),
        name="matmul_residual",
    )(a2d, w.astype(jnp.bfloat16), res2d)


PEER_TOKENS_PER_STEP = 8
PEER_ROWS_PER_TOKEN = PEER_HEADS * PEER_TOPK
PEER_ROWS_PER_STEP = PEER_TOKENS_PER_STEP * PEER_ROWS_PER_TOKEN
SUBLANES = 8
PEER_GROUPS_PER_TOKEN = PEER_ROWS_PER_TOKEN // SUBLANES
PEER_GROUPS_PER_STEP = PEER_ROWS_PER_STEP // SUBLANES


def _peer_gather_kernel(e_hbm, x_ref, gain_ref, gt_ref, uv_hbm, o_ref,
                        idx_smem, buf, idx_sem, row_sem):
    i = pl.program_id(0)
    n = pl.num_programs(0)
    slot = i % 2
    nslot = 1 - slot
    R = PEER_ROWS_PER_STEP

    def idx_copy(step, s):
        return pltpu.make_async_copy(e_hbm.at[pl.ds(step * R, R)],
                                     idx_smem.at[pl.ds(s * R, R)], idx_sem.at[s])

    def issue_token_rows(s, t):
        def body(gi, carry):
            grp = t * PEER_GROUPS_PER_TOKEN + gi
            for c in range(SUBLANES):
                idx = idx_smem[s * R + grp * SUBLANES + c]
                pltpu.make_async_copy(
                    uv_hbm.at[lax.shift_right_logical(idx, 3), pl.ds(idx & (SUBLANES - 1), 1)],
                    buf.at[s, grp, pl.ds(c, 1)],
                    row_sem.at[s]).start(priority=c % 2)
            return carry
        lax.fori_loop(0, PEER_GROUPS_PER_TOKEN, body, 0)

    @pl.when(i == 0)
    def _():
        idx_copy(0, 0).start()
        idx_copy(0, 0).wait()
        for t in range(PEER_TOKENS_PER_STEP):
            issue_token_rows(0, t)

        @pl.when(n > 1)
        def _():
            idx_copy(1, 1).start()

    has_next = i + 1 < n

    @pl.when(has_next)
    def _():
        idx_copy(i + 1, nslot).wait()

    @pl.when(i + 2 < n)
    def _():
        idx_copy(i + 2, slot).start()

    @pl.when(has_next)
    def _():
        issue_token_rows(nslot, 0)

    pltpu.make_async_copy(uv_hbm.at[pl.ds(0, PEER_GROUPS_PER_STEP)], buf.at[slot],
                          row_sem.at[slot]).wait()

    x = x_ref[...]
    xn = x * lax.rsqrt(jnp.mean(x * x, axis=-1, keepdims=True) + RMS_EPS) * gain_ref[...]
    outs = []
    for t in range(PEER_TOKENS_PER_STEP):
        rows = buf[slot, pl.ds(t * PEER_GROUPS_PER_TOKEN, PEER_GROUPS_PER_TOKEN)]
        rows = rows.reshape(PEER_ROWS_PER_TOKEN, 2 * D_MODEL)
        u = rows[:, 0:D_MODEL]
        v = rows[:, D_MODEL:2 * D_MODEL]
        h = jnp.sum(u * xn[t:t + 1, :], axis=-1, keepdims=True)
        a = jax.nn.gelu(h) * gt_ref[:, t:t + 1]
        outs.append(jnp.sum(a * v, axis=0, keepdims=True))
        if t + 1 < PEER_TOKENS_PER_STEP:
            @pl.when(has_next)
            def _():
                issue_token_rows(nslot, t + 1)
    o_ref[...] = x + jnp.concatenate(outs, axis=0)


def peer_gather(x2d, gain, e, g, uv):
    T, D = x2d.shape
    TB = PEER_TOKENS_PER_STEP
    R = PEER_ROWS_PER_STEP
    n_steps = T // TB
    gt = g.reshape(n_steps, TB, PEER_ROWS_PER_TOKEN).transpose(0, 2, 1)
    uv3 = uv.reshape(uv.shape[0] // SUBLANES, SUBLANES, 2 * D)
    return pl.pallas_call(
        _peer_gather_kernel,
        grid=(n_steps,),
        in_specs=[
            pl.BlockSpec(memory_space=pl.ANY),
            pl.BlockSpec((TB, D), lambda i: (i, 0)),
            pl.BlockSpec((1, D), lambda i: (0, 0)),
            pl.BlockSpec((None, PEER_ROWS_PER_TOKEN, TB), lambda i: (i, 0, 0)),
            pl.BlockSpec(memory_space=pl.ANY),
        ],
        out_specs=pl.BlockSpec((TB, D), lambda i: (i, 0)),
        out_shape=jax.ShapeDtypeStruct((T, D), jnp.float32),
        scratch_shapes=[
            pltpu.SMEM((2 * R,), jnp.int32),
            pltpu.VMEM((2, PEER_GROUPS_PER_STEP, SUBLANES, 2 * D), jnp.float32),
            pltpu.SemaphoreType.DMA((2,)),
            pltpu.SemaphoreType.DMA((2,)),
        ],
        compiler_params=pltpu.CompilerParams(
            dimension_semantics=("arbitrary",), vmem_limit_bytes=VMEM_LIMIT_BYTES),
        name="peer_gather",
    )(e.reshape(T * PEER_ROWS_PER_TOKEN), x2d, gain.reshape(1, D), gt, uv3)


def _rmsnorm(x, g):
    y = x * lax.rsqrt(jnp.mean(x * x, axis=-1, keepdims=True) + RMS_EPS)
    return y * g


def _rope(x, pos):
    half = ROPE_DIM // 2
    freqs = ROPE_THETA ** (-jnp.arange(half, dtype=jnp.float32) / half)
    ang = pos.astype(jnp.float32)[:, None] * freqs[None, :]
    cos = jnp.cos(ang)[:, None, :]
    sin = jnp.sin(ang)[:, None, :]
    x1, x2, rest = x[..., :half], x[..., half:ROPE_DIM], x[..., ROPE_DIM:]
    return jnp.concatenate([x1 * cos - x2 * sin, x2 * cos + x1 * sin, rest], axis=-1)


def _masked_softmax(s, mask):
    s = jnp.where(mask, s, NEG_INF)
    p = jax.nn.softmax(s, axis=-1)
    return jnp.where(mask, p, 0.0)


def _causal_conv(x, w, b):
    K, C = w.shape
    y = lax.conv_general_dilated(x, w[:, None, :], window_strides=(1,),
                                 padding=[(K - 1, 0)], dimension_numbers=('NWC', 'WIO', 'NWC'),
                                 feature_group_count=C)
    return y + b


def _rg_lru(x, gate_w, gate_b, lam):
    B, S, C = x.shape
    xb = x.reshape(B, S, LRU_BLOCKS, HEAD_DIM)
    gates = jnp.einsum('bsnd,gnde->gbsne', xb, gate_w).reshape(2, B, S, C)
    gates = gates + gate_b[:, None, None, :]
    r = jax.nn.sigmoid(gates[0])
    i = jax.nn.sigmoid(gates[1])
    log_a = -LRU_C * r * jax.nn.softplus(-lam)
    a = jnp.exp(log_a)
    u = jnp.sqrt(-jnp.expm1(2.0 * log_a)) * (i * x)

    def combine(left, right):
        a1, b1 = left
        a2, b2 = right
        return a1 * a2, a2 * b1 + b2

    _, h = lax.associative_scan(combine, (a, u), axis=1)
    return h


def _memory_attention(qm, mem, norm_g, w_kv, q_gain, k_gain):
    B, S, _ = qm.shape
    M = mem.shape[1]
    kv = norm_matmul(mem.reshape(B * M, D_MODEL), norm_g, w_kv).reshape(B, M, 2, MEM_HEADS, HEAD_DIM)
    q = _rmsnorm(qm.reshape(B, S, MEM_HEADS, HEAD_DIM), q_gain)
    k = _rmsnorm(kv[:, :, 0], k_gain)
    v = kv[:, :, 1]
    s = jnp.einsum('bshd,bmhd->bhsm', q, k) * (HEAD_DIM ** -0.5)
    p = jax.nn.softmax(s, axis=-1)
    return jnp.einsum('bhsm,bmhd->bshd', p, v).reshape(B, S, MEM_WIDTH)


def _nsa_shared_kv(x, positions, kv_norm, w_kv_shared, k_gain_shared, cmp_pos, cmp_w1, cmp_b1, cmp_w2):
    B, S, _ = x.shape
    G, dh = NSA_KV_HEADS, HEAD_DIM
    kv = norm_matmul(x.reshape(B * S, D_MODEL), kv_norm, w_kv_shared).reshape(B, S, N_BRANCH, 2, G, dh)
    n_cmp = (S - CMP_LEN) // CMP_STRIDE + 1
    idx = np.arange(n_cmp)[:, None] * CMP_STRIDE + np.arange(CMP_LEN)[None, :]

    def compress(tok, j):
        blk = tok[:, idx] + cmp_pos[j][None, None, :, None, :]
        blk = blk.transpose(0, 1, 3, 2, 4).reshape(B, n_cmp, G, CMP_LEN * dh)
        hid = jax.nn.gelu(blk @ cmp_w1[j] + cmp_b1[j])
        return hid @ cmp_w2[j]

    kc = _rope(_rmsnorm(compress(kv[:, :, 0, 0], 0), k_gain_shared[0]), positions[idx[:, -1]])
    vc = compress(kv[:, :, 0, 1], 1)
    ks = _rope(_rmsnorm(kv[:, :, 1, 0], k_gain_shared[1]), positions)
    vs = kv[:, :, 1, 1]
    kw = _rope(_rmsnorm(kv[:, :, 2, 0], k_gain_shared[2]), positions)
    vw = kv[:, :, 2, 1]
    t = lambda a: a.transpose(0, 2, 1, 3)
    return (t(kc), t(vc), t(ks), t(vs), t(kw), t(vw))


def _cmp_sel_overlap(n_cmp, n_sel):
    c0 = np.arange(n_cmp)[:, None] * CMP_STRIDE
    s0 = np.arange(n_sel)[None, :] * SEL_LEN
    ov = np.minimum(c0 + CMP_LEN, s0 + SEL_LEN) - np.maximum(c0, s0)
    return (np.clip(ov, 0, None) / CMP_LEN).astype(np.float32)


def _nsa_attention(q, kc, vc, ks, vs, kw, vw):
    B, S, H, dh = q.shape
    G = kc.shape[1]
    R = H // G
    C = Q_CHUNK
    NQ = S // C
    n_cmp = kc.shape[2]
    n_sel = S // SEL_LEN
    top = min(SEL_TOP, n_sel)
    scale = dh ** -0.5
    cmp_end = jnp.arange(n_cmp) * CMP_STRIDE + CMP_LEN - 1
    overlap = jnp.asarray(_cmp_sel_overlap(n_cmp, n_sel))
    ks_blk = ks.reshape(B, G, n_sel, SEL_LEN, dh)
    vs_blk = vs.reshape(B, G, n_sel, SEL_LEN, dh)
    kw_pad = jnp.pad(kw, ((0, 0), (0, 0), (WINDOW, 0), (0, 0)))
    vw_pad = jnp.pad(vw, ((0, 0), (0, 0), (WINDOW, 0), (0, 0)))
    q_chunks = q.reshape(B, NQ, C, G, R, dh).transpose(1, 0, 3, 4, 2, 5)
    b_ix = jnp.arange(B)[:, None, None, None]
    g_ix = jnp.arange(G)[None, :, None, None]
    blk_ids = jnp.arange(n_sel)

    def chunk(args):
        qc, c = args
        t = c * C + jnp.arange(C)
        s = jnp.einsum('bgrqd,bgnd->bgrqn', qc, kc) * scale
        p_cmp = _masked_softmax(s, cmp_end[None, :] <= t[:, None])
        o_cmp = jnp.einsum('bgrqn,bgnd->bgrqd', p_cmp, vc)
        imp = jnp.einsum('bgrqn,nj->bgqj', p_cmp, overlap)
        cur = t // SEL_LEN
        valid = blk_ids[None, :] <= cur[:, None]
        forced = (blk_ids[None, :] == 0) | (blk_ids[None, :] == cur[:, None]) | (blk_ids[None, :] == cur[:, None] - 1)
        imp = jnp.where(forced, SEL_FORCE_SCORE, imp)
        imp = jnp.where(valid, imp, -1.0)
        vals, idx = lax.top_k(imp, top)
        kg = ks_blk[b_ix, g_ix, idx]
        vg = vs_blk[b_ix, g_ix, idx].reshape(B, G, C, top * SEL_LEN, dh)
        s = jnp.einsum('bgrqd,bgqkld->bgrqkl', qc, kg) * scale
        kpos = idx[..., None] * SEL_LEN + jnp.arange(SEL_LEN)
        m = (vals >= 0.0)[..., None] & (kpos <= t[None, None, :, None, None])
        p_sel = _masked_softmax(s.reshape(B, G, R, C, top * SEL_LEN),
                                m.reshape(B, G, 1, C, top * SEL_LEN))
        o_sel = jnp.einsum('bgrqk,bgqkd->bgrqd', p_sel, vg)
        kwin = lax.dynamic_slice_in_dim(kw_pad, c * C, WINDOW + C, axis=2)
        vwin = lax.dynamic_slice_in_dim(vw_pad, c * C, WINDOW + C, axis=2)
        wpos = c * C - WINDOW + jnp.arange(WINDOW + C)
        m = (wpos[None, :] <= t[:, None]) & (wpos[None, :] > t[:, None] - WINDOW) & (wpos[None, :] >= 0)
        s = jnp.einsum('bgrqd,bgkd->bgrqk', qc, kwin) * scale
        p_win = _masked_softmax(s, m)
        o_win = jnp.einsum('bgrqk,bgkd->bgrqd', p_win, vwin)
        return (o_cmp, o_sel, o_win)

    outs = lax.map(chunk, (q_chunks, jnp.arange(NQ)))
    back = lambda o: o.transpose(1, 0, 4, 2, 3, 5).reshape(B, S, H, dh)
    return (back(outs[0]), back(outs[1]), back(outs[2]))


def _peer_route(q_all, sub_keys):
    T = q_all.shape[0]
    half = PEER_KEY_DIM // 2
    q = q_all.reshape(T, PEER_HEADS, 2, half)
    s = jnp.einsum('thpd,hpnd->thpn', q, sub_keys)
    v1, i1 = lax.top_k(s[:, :, 0], PEER_TOPK)
    v2, i2 = lax.top_k(s[:, :, 1], PEER_TOPK)
    cand = (v1[..., :, None] + v2[..., None, :]).reshape(T, PEER_HEADS, PEER_TOPK * PEER_TOPK)
    vals, ci = lax.top_k(cand, PEER_TOPK)
    e = (jnp.take_along_axis(i1, ci // PEER_TOPK, axis=-1) * N_KEYS
         + jnp.take_along_axis(i2, ci % PEER_TOPK, axis=-1))
    g = jax.nn.softmax(vals, axis=-1)
    return (e.reshape(T, PEER_ROWS_PER_TOKEN).astype(jnp.int32),
            g.reshape(T, PEER_ROWS_PER_TOKEN))


def _peer_residual(x2d, norm_g, w_q, sub_keys, u_tab, v_tab):
    q_all = norm_matmul(x2d, norm_g, w_q)
    e, g = _peer_route(q_all, sub_keys)
    uv = jnp.concatenate([u_tab, v_tab], axis=1)
    return peer_gather(x2d, norm_g, e, g, uv)


def kernel(x, mem, positions, norm_mix, norm_ffn, norm_mem, w_out, w_mem_kv, mem_q_gain, mem_k_gain, a_w_in, a_conv_w, a_conv_b, a_gate_w, a_gate_b, a_lambda, b_w_in, b_gate_b, b_q_gain, kv_norm, w_kv_shared, k_gain_shared, cmp_pos, cmp_w1, cmp_b1, cmp_w2, peer_wq, peer_subkeys, peer_u, peer_v):
    B, S, D = x.shape
    T = B * S
    x2d = x.reshape(T, D)
    shared = None
    for l in range(DEPTH):
        if l < N_A:
            proj = norm_matmul(x2d, norm_mix[l], a_w_in[l]).reshape(B, S, -1)
            xb = _causal_conv(proj[..., :LRU_WIDTH], a_conv_w[l], a_conv_b[l])
            yb = proj[..., LRU_WIDTH:2 * LRU_WIDTH]
            qm = proj[..., 2 * LRU_WIDTH:]
            mix = _rg_lru(xb, a_gate_w[l], a_gate_b[l], a_lambda[l]) * jax.nn.gelu(yb)
        else:
            j = l - N_A
            proj = norm_matmul(x2d, norm_mix[l], b_w_in[j]).reshape(B, S, -1)
            q = proj[..., :NSA_WIDTH].reshape(B, S, NSA_HEADS, HEAD_DIM)
            gl = (proj[..., NSA_WIDTH:NSA_WIDTH + N_BRANCH * NSA_HEADS].reshape(B, S, NSA_HEADS, N_BRANCH)
                  + b_gate_b[j].reshape(NSA_HEADS, N_BRANCH))
            qm = proj[..., NSA_WIDTH + N_BRANCH * NSA_HEADS:]
            q = _rope(_rmsnorm(q, b_q_gain[j]), positions)
            o_cmp, o_sel, o_win = _nsa_attention(q, *shared)
            g = jax.nn.sigmoid(gl)
            mix = (g[..., 0:1] * o_cmp + g[..., 1:2] * o_sel + g[..., 2:3] * o_win).reshape(B, S, NSA_WIDTH)
        mo = _memory_attention(qm, mem, norm_mem[l], w_mem_kv[l], mem_q_gain[l], mem_k_gain[l])
        cat = jnp.concatenate([mix, mo], axis=-1).reshape(T, MIX_WIDTH)
        x2d = matmul_residual(cat, w_out[l], x2d)
        x2d = _peer_residual(x2d, norm_ffn[l], peer_wq[l], peer_subkeys[l], peer_u[l], peer_v[l])
        if l == N_A - 1:
            shared = _nsa_shared_kv(x2d.reshape(B, S, D), positions, kv_norm, w_kv_shared, k_gain_shared,
                                    cmp_pos, cmp_w1, cmp_b1, cmp_w2)
    return x2d.reshape(B, S, D)
```

```python
import functools
import math

import jax
import jax.numpy as jnp
import numpy as np
from jax import lax
from jax.experimental import pallas as pl
from jax.experimental.pallas import tpu as pltpu

D_MODEL = 1024
DEPTH = 2
N_A = DEPTH // 2
HEAD_DIM = 64
ROPE_DIM = HEAD_DIM // 4
ROPE_THETA = 500000.0
RMS_EPS = 1e-6
NEG_INF = -1e30
MEM_HEADS = 4
MEM_WIDTH = MEM_HEADS * HEAD_DIM
MIX_WIDTH = D_MODEL
LRU_WIDTH = MIX_WIDTH - MEM_WIDTH
LRU_BLOCKS = LRU_WIDTH // HEAD_DIM
CONV_WIDTH = 4
LRU_C = 8.0
NSA_WIDTH = MIX_WIDTH - MEM_WIDTH
NSA_HEADS = NSA_WIDTH // HEAD_DIM
NSA_KV_HEADS = 2
N_BRANCH = 3
CMP_LEN = 32
CMP_STRIDE = 16
SEL_LEN = 64
SEL_TOP = 16
SEL_FORCE_SCORE = 1e4
WINDOW = 512
Q_CHUNK = 64
N_KEYS = 128
PEER_HEADS = 8
PEER_KEY_DIM = 256
PEER_TOPK = 16
PEER_CHUNK_MAX = 512

VMEM_LIMIT_BYTES = 48 * 1024 * 1024
ROW_TILE = 512


def _norm_matmul_kernel(x_ref, g_ref, w_ref, o_ref):
    x = x_ref[...]
    ms = jnp.mean(x * x, axis=-1, keepdims=True)
    y = x * lax.rsqrt(ms + RMS_EPS) * g_ref[...]
    o_ref[...] = jnp.dot(y.astype(jnp.bfloat16), w_ref[...],
                         preferred_element_type=jnp.float32)


def norm_matmul(x2d, g, w):
    T, D = x2d.shape
    N = w.shape[1]
    return pl.pallas_call(
        _norm_matmul_kernel,
        grid=(T // ROW_TILE,),
        in_specs=[
            pl.BlockSpec((ROW_TILE, D), lambda i: (i, 0)),
            pl.BlockSpec((1, D), lambda i: (0, 0)),
            pl.BlockSpec((D, N), lambda i: (0, 0)),
        ],
        out_specs=pl.BlockSpec((ROW_TILE, N), lambda i: (i, 0)),
        out_shape=jax.ShapeDtypeStruct((T, N), jnp.float32),
        compiler_params=pltpu.CompilerParams(
            dimension_semantics=("arbitrary",), vmem_limit_bytes=VMEM_LIMIT_BYTES),
        name="norm_matmul",
    )(x2d, g.reshape(1, D), w.astype(jnp.bfloat16))


def _matmul_residual_kernel(a_ref, w_ref, r_ref, o_ref):
    o_ref[...] = r_ref[...] + jnp.dot(a_ref[...].astype(jnp.bfloat16), w_ref[...],
                                      preferred_element_type=jnp.float32)


def matmul_residual(a2d, w, res2d):
    T, K = a2d.shape
    N = w.shape[1]
    return pl.pallas_call(
        _matmul_residual_kernel,
        grid=(T // ROW_TILE,),
        in_specs=[
            pl.BlockSpec((ROW_TILE, K), lambda i: (i, 0)),
            pl.BlockSpec((K, N), lambda i: (0, 0)),
            pl.BlockSpec((ROW_TILE, N), lambda i: (i, 0)),
        ],
        out_specs=pl.BlockSpec((ROW_TILE, N), lambda i: (i, 0)),
        out_shape=jax.ShapeDtypeStruct((T, N), jnp.float32),
        compiler_params=pltpu.CompilerParams(
            dimension_semantics=("arbitrary",), vmem_limit_bytes=VMEM_LIMIT_BYTES),
        name="matmul_residual",
    )(a2d, w.astype(jnp.bfloat16), res2d)


PEER_TOKENS_PER_STEP = 8
PEER_ROWS_PER_TOKEN = PEER_HEADS * PEER_TOPK
PEER_ROWS_PER_STEP = PEER_TOKENS_PER_STEP * PEER_ROWS_PER_TOKEN
SUBLANES = 8
PEER_GROUPS_PER_TOKEN = PEER_ROWS_PER_TOKEN // SUBLANES
PEER_GROUPS_PER_STEP = PEER_ROWS_PER_STEP // SUBLANES


def _peer_gather_kernel(e_hbm, x_ref, gain_ref, gt_ref, uv_hbm, o_ref,
                        idx_smem, buf, idx_sem, row_sem):
    i = pl.program_id(0)
    n = pl.num_programs(0)
    slot = i % 2
    nslot = 1 - slot
    R = PEER_ROWS_PER_STEP

    def idx_copy(step, s):
        return pltpu.make_async_copy(e_hbm.at[pl.ds(step * R, R)],
                                     idx_smem.at[pl.ds(s * R, R)], idx_sem.at[s])

    def issue_token_rows(s, t):
        def body(gi, carry):
            grp = t * PEER_GROUPS_PER_TOKEN + gi
            for c in range(SUBLANES):
                idx = idx_smem[s * R + grp * SUBLANES + c]
                pltpu.make_async_copy(
                    uv_hbm.at[lax.shift_right_logical(idx, 3), pl.ds(idx & (SUBLANES - 1), 1)],
                    buf.at[s, grp, pl.ds(c, 1)],
                    row_sem.at[s]).start(priority=c % 2)
            return carry
        lax.fori_loop(0, PEER_GROUPS_PER_TOKEN, body, 0)

    @pl.when(i == 0)
    def _():
        idx_copy(0, 0).start()
        idx_copy(0, 0).wait()
        for t in range(PEER_TOKENS_PER_STEP):
            issue_token_rows(0, t)

        @pl.when(n > 1)
        def _():
            idx_copy(1, 1).start()

    has_next = i + 1 < n

    @pl.when(has_next)
    def _():
        idx_copy(i + 1, nslot).wait()

    @pl.when(i + 2 < n)
    def _():
        idx_copy(i + 2, slot).start()

    @pl.when(has_next)
    def _():
        issue_token_rows(nslot, 0)

    pltpu.make_async_copy(uv_hbm.at[pl.ds(0, PEER_GROUPS_PER_STEP)], buf.at[slot],
                          row_sem.at[slot]).wait()

    x = x_ref[...]
    xn = x * lax.rsqrt(jnp.mean(x * x, axis=-1, keepdims=True) + RMS_EPS) * gain_ref[...]
    outs = []
    for t in range(PEER_TOKENS_PER_STEP):
        rows = buf[slot, pl.ds(t * PEER_GROUPS_PER_TOKEN, PEER_GROUPS_PER_TOKEN)]
        rows = rows.reshape(PEER_ROWS_PER_TOKEN, 2 * D_MODEL)
        u = rows[:, 0:D_MODEL]
        v = rows[:, D_MODEL:2 * D_MODEL]
        h = jnp.sum(u * xn[t:t + 1, :], axis=-1, keepdims=True)
        a = jax.nn.gelu(h) * gt_ref[:, t:t + 1]
        outs.append(jnp.sum(a * v, axis=0, keepdims=True))
        if t + 1 < PEER_TOKENS_PER_STEP:
            @pl.when(has_next)
            def _():
                issue_token_rows(nslot, t + 1)
    o_ref[...] = x + jnp.concatenate(outs, axis=0)


def peer_gather(x2d, gain, e, g, uv):
    T, D = x2d.shape
    TB = PEER_TOKENS_PER_STEP
    R = PEER_ROWS_PER_STEP
    n_steps = T // TB
    gt = g.reshape(n_steps, TB, PEER_ROWS_PER_TOKEN).transpose(0, 2, 1)
    uv3 = uv.reshape(uv.shape[0] // SUBLANES, SUBLANES, 2 * D)
    return pl.pallas_call(
        _peer_gather_kernel,
        grid=(n_steps,),
        in_specs=[
            pl.BlockSpec(memory_space=pl.ANY),
            pl.BlockSpec((TB, D), lambda i: (i, 0)),
            pl.BlockSpec((1, D), lambda i: (0, 0)),
            pl.BlockSpec((None, PEER_ROWS_PER_TOKEN, TB), lambda i: (i, 0, 0)),
            pl.BlockSpec(memory_space=pl.ANY),
        ],
        out_specs=pl.BlockSpec((TB, D), lambda i: (i, 0)),
        out_shape=jax.ShapeDtypeStruct((T, D), jnp.float32),
        scratch_shapes=[
            pltpu.SMEM((2 * R,), jnp.int32),
            pltpu.VMEM((2, PEER_GROUPS_PER_STEP, SUBLANES, 2 * D), jnp.float32),
            pltpu.SemaphoreType.DMA((2,)),
            pltpu.SemaphoreType.DMA((2,)),
        ],
        compiler_params=pltpu.CompilerParams(
            dimension_semantics=("arbitrary",), vmem_limit_bytes=VMEM_LIMIT_BYTES),
        name="peer_gather",
    )(e.reshape(T * PEER_ROWS_PER_TOKEN), x2d, gain.reshape(1, D), gt, uv3)


PEER_ROUTE_TOKENS = 256
PEER_HALF = PEER_KEY_DIM // 2


def _top16_rows(s):
    n = s.shape[0]
    row = lax.broadcasted_iota(jnp.int32, s.shape, 0)
    vals, idxs = [], []
    for _ in range(PEER_TOPK):
        m = jnp.max(s, axis=0, keepdims=True)
        idx = jnp.min(jnp.where(s == m, row, n), axis=0, keepdims=True)
        vals.append(m)
        idxs.append(idx)
        s = jnp.where(row == idx, -jnp.inf, s)
    return jnp.concatenate(vals, axis=0), jnp.concatenate(idxs, axis=0)


def _pick_rows(table, which):
    out = jnp.zeros_like(table)
    for a in range(PEER_TOPK):
        out = out + jnp.where(which == a, table[a:a + 1, :], 0)
    return out


def _peer_route_kernel(q_ref, keys_ref, e_ref, g_ref):
    qb = q_ref[...].astype(jnp.bfloat16)
    s1 = lax.dot_general(keys_ref[0], qb[:, :PEER_HALF], (((1,), (1,)), ((), ())),
                         preferred_element_type=jnp.float32)
    s2 = lax.dot_general(keys_ref[1], qb[:, PEER_HALF:], (((1,), (1,)), ((), ())),
                         preferred_element_type=jnp.float32)
    v1, i1 = _top16_rows(s1)
    v2, i2 = _top16_rows(s2)
    cand = jnp.concatenate([v1[a:a + 1, :] + v2 for a in range(PEER_TOPK)], axis=0)
    vals, ci = _top16_rows(cand)
    e_ref[...] = (_pick_rows(i1, lax.shift_right_logical(ci, 4)) * N_KEYS
                  + _pick_rows(i2, ci & (PEER_TOPK - 1)))
    p = jnp.exp(vals - vals[0:1, :])
    g_ref[...] = p / jnp.sum(p, axis=0, keepdims=True)


def peer_route(q_all, sub_keys):
    T = q_all.shape[0]
    TT = PEER_ROUTE_TOKENS
    return pl.pallas_call(
        _peer_route_kernel,
        grid=(T // TT, PEER_HEADS),
        in_specs=[
            pl.BlockSpec((TT, PEER_KEY_DIM), lambda i, h: (i, h)),
            pl.BlockSpec((None, 2, N_KEYS, PEER_HALF), lambda i, h: (h, 0, 0, 0)),
        ],
        out_specs=[
            pl.BlockSpec((PEER_TOPK, TT), lambda i, h: (h, i)),
            pl.BlockSpec((PEER_TOPK, TT), lambda i, h: (h, i)),
        ],
        out_shape=[jax.ShapeDtypeStruct((PEER_ROWS_PER_TOKEN, T), jnp.int32),
                   jax.ShapeDtypeStruct((PEER_ROWS_PER_TOKEN, T), jnp.float32)],
        compiler_params=pltpu.CompilerParams(
            dimension_semantics=("arbitrary", "arbitrary"), vmem_limit_bytes=VMEM_LIMIT_BYTES),
        name="peer_route",
    )(q_all, sub_keys.astype(jnp.bfloat16))


NSA_GROUP = NSA_HEADS // NSA_KV_HEADS
NSA_ROWS = NSA_GROUP * Q_CHUNK
NSA_KEY_TILE = 256


def _qk(qb, k):
    return lax.dot_general(qb, k, (((1,), (1,)), ((), ())), preferred_element_type=jnp.float32)


def _nsa_kernel(q_ref, gl_ref, kc_ref, vc_ref, ov_ref, ks_ref, vs_ref, kw_ref, vw_ref,
                o_ref, mexp_ref):
    c = pl.program_id(1)
    rows = NSA_ROWS
    n_sel = mexp_ref.shape[0] * (NSA_KEY_TILE // SEL_LEN)
    n_cmp_pad = kc_ref.shape[0]
    f32 = jnp.float32
    bf16 = jnp.bfloat16

    qb = (q_ref[...].reshape(rows, HEAD_DIM) * (HEAD_DIM ** -0.5)).astype(bf16)
    t = c * Q_CHUNK + (lax.broadcasted_iota(jnp.int32, (rows, 1), 0) & (Q_CHUNK - 1))

    s = _qk(qb, kc_ref[...])
    cmp_end = lax.broadcasted_iota(jnp.int32, (1, n_cmp_pad), 1) * CMP_STRIDE + (CMP_LEN - 1)
    mask = cmp_end <= t
    m = jnp.max(jnp.where(mask, s, NEG_INF), axis=-1, keepdims=True)
    e = jnp.where(mask, jnp.exp(s - m), 0.0)
    l = jnp.sum(e, axis=-1, keepdims=True)
    pb = (e / jnp.where(l > 0.0, l, 1.0)).astype(bf16)
    o_cmp = jnp.dot(pb, vc_ref[...], preferred_element_type=f32)

    imp = jnp.dot(pb, ov_ref[...], preferred_element_type=f32)
    imp = jnp.sum(imp.reshape(NSA_GROUP, Q_CHUNK, n_sel), axis=0)
    j = lax.broadcasted_iota(jnp.int32, (Q_CHUNK, n_sel), 1)
    forced = (j == 0) | (j == c) | (j == c - 1)
    imp = jnp.where(forced, SEL_FORCE_SCORE, imp)
    imp = jnp.where(j <= c, imp, -1.0)
    rank = jnp.zeros((Q_CHUNK, n_sel), f32)
    for jp in range(n_sel):
        col = imp[:, jp:jp + 1]
        beats = (col > imp) | ((col == imp) & (j > jp))
        rank = rank + jnp.where(beats, 1.0, 0.0)
    sel = jnp.where((rank < float(SEL_TOP)) & (imp >= 0.0), 1.0, 0.0).astype(bf16)
    blk_of_key = lax.broadcasted_iota(jnp.int32, (n_sel, NSA_KEY_TILE), 1) // SEL_LEN
    blk_row = lax.broadcasted_iota(jnp.int32, (n_sel, NSA_KEY_TILE), 0)
    for kt in range(mexp_ref.shape[0]):
        expand = jnp.where(blk_of_key + kt * (NSA_KEY_TILE // SEL_LEN) == blk_row, 1.0, 0.0).astype(bf16)
        mexp_ref[kt] = jnp.dot(sel, expand, preferred_element_type=f32)

    def sel_tile(kt, carry):
        m_i, l_i, acc = carry
        off = pl.multiple_of(kt * NSA_KEY_TILE, NSA_KEY_TILE)
        s = _qk(qb, ks_ref[pl.ds(off, NSA_KEY_TILE), :])
        kpos = off + lax.broadcasted_iota(jnp.int32, (1, NSA_KEY_TILE), 1)
        picked = jnp.concatenate([mexp_ref[kt]] * NSA_GROUP, axis=0) > 0.5
        mask = picked & (kpos <= t)
        m_new = jnp.maximum(m_i, jnp.max(jnp.where(mask, s, NEG_INF), axis=-1, keepdims=True))
        alpha = jnp.exp(m_i - m_new)
        p = jnp.where(mask, jnp.exp(s - m_new), 0.0)
        l_new = alpha * l_i + jnp.sum(p, axis=-1, keepdims=True)
        acc_new = alpha * acc + jnp.dot(p.astype(bf16), vs_ref[pl.ds(off, NSA_KEY_TILE), :],
                                        preferred_element_type=f32)
        return m_new, l_new, acc_new

    n_tiles = c // (NSA_KEY_TILE // SEL_LEN) + 1
    init = (jnp.full((rows, 1), NEG_INF, f32), jnp.zeros((rows, 1), f32),
            jnp.zeros((rows, HEAD_DIM), f32))
    _, l_s, acc_s = lax.fori_loop(0, n_tiles, sel_tile, init)
    o_sel = acc_s / l_s

    start = pl.multiple_of(jnp.maximum(c * Q_CHUNK - WINDOW, 0), Q_CHUNK)
    cur = pl.multiple_of(c * Q_CHUNK, Q_CHUNK)
    s_a = _qk(qb, kw_ref[pl.ds(start, WINDOW), :])
    s_b = _qk(qb, kw_ref[pl.ds(cur, Q_CHUNK), :])
    pos_a = start + lax.broadcasted_iota(jnp.int32, (1, WINDOW), 1)
    pos_b = cur + lax.broadcasted_iota(jnp.int32, (1, Q_CHUNK), 1)
    mask_a = (pos_a < cur) & (pos_a > t - WINDOW)
    mask_b = pos_b <= t
    m_w = jnp.maximum(jnp.max(jnp.where(mask_a, s_a, NEG_INF), axis=-1, keepdims=True),
                      jnp.max(jnp.where(mask_b, s_b, NEG_INF), axis=-1, keepdims=True))
    p_a = jnp.where(mask_a, jnp.exp(s_a - m_w), 0.0)
    p_b = jnp.where(mask_b, jnp.exp(s_b - m_w), 0.0)
    l_w = jnp.sum(p_a, axis=-1, keepdims=True) + jnp.sum(p_b, axis=-1, keepdims=True)
    o_win = (jnp.dot(p_a.astype(bf16), vw_ref[pl.ds(start, WINDOW), :], preferred_element_type=f32)
             + jnp.dot(p_b.astype(bf16), vw_ref[pl.ds(cur, Q_CHUNK), :], preferred_element_type=f32)) / l_w

    g = jax.nn.sigmoid(gl_ref[...])
    o = g[:, 0:1] * o_cmp + g[:, 1:2] * o_sel + g[:, 2:3] * o_win
    o_ref[...] = o.reshape(NSA_GROUP, Q_CHUNK, HEAD_DIM)


def nsa_attention(q, gl, kc, vc, ks, vs, kw, vw):
    B, S, H, dh = q.shape
    G = NSA_KV_HEADS
    R = NSA_GROUP
    NQ = S // Q_CHUNK
    n_sel = S // SEL_LEN
    n_cmp = kc.shape[2]
    n_cmp_pad = -(-n_cmp // 128) * 128
    bf16 = jnp.bfloat16
    qt = q.reshape(B, S, G, R, dh).transpose(0, 2, 3, 1, 4).reshape(B * G, R, S, dh)
    glt = gl.reshape(B, NQ, Q_CHUNK, G, R, N_BRANCH).transpose(0, 3, 1, 4, 2, 5)
    glt = glt.reshape(B * G, NQ, NSA_ROWS, N_BRANCH)
    pad_c = lambda a: jnp.pad(a.reshape(B * G, n_cmp, dh), ((0, 0), (0, n_cmp_pad - n_cmp), (0, 0))).astype(bf16)
    flat = lambda a: a.reshape(B * G, S, dh).astype(bf16)
    c0 = np.arange(n_cmp_pad)[:, None] * CMP_STRIDE
    s0 = np.arange(n_sel)[None, :] * SEL_LEN
    ov = np.clip(np.minimum(c0 + CMP_LEN, s0 + SEL_LEN) - np.maximum(c0, s0), 0, None) / CMP_LEN
    ov[n_cmp:] = 0.0
    kv_spec = pl.BlockSpec((None, S, dh), lambda bg, c: (bg, 0, 0))
    cmp_spec = pl.BlockSpec((None, n_cmp_pad, dh), lambda bg, c: (bg, 0, 0))
    out = pl.pallas_call(
        _nsa_kernel,
        grid=(B * G, NQ),
        in_specs=[
            pl.BlockSpec((None, R, Q_CHUNK, dh), lambda bg, c: (bg, 0, c, 0)),
            pl.BlockSpec((None, None, NSA_ROWS, N_BRANCH), lambda bg, c: (bg, c, 0, 0)),
            cmp_spec, cmp_spec,
            pl.BlockSpec((n_cmp_pad, n_sel), lambda bg, c: (0, 0)),
            kv_spec, kv_spec, kv_spec, kv_spec,
        ],
        out_specs=pl.BlockSpec((None, R, Q_CHUNK, dh), lambda bg, c: (bg, 0, c, 0)),
        out_shape=jax.ShapeDtypeStruct((B * G, R, S, dh), jnp.float32),
        scratch_shapes=[pltpu.VMEM((S // NSA_KEY_TILE, Q_CHUNK, NSA_KEY_TILE), jnp.float32)],
        compiler_params=pltpu.CompilerParams(
            dimension_semantics=("arbitrary", "arbitrary"), vmem_limit_bytes=VMEM_LIMIT_BYTES),
        name="nsa_attention",
    )(qt, glt, pad_c(kc), pad_c(vc), jnp.asarray(ov, bf16), flat(ks), flat(vs), flat(kw), flat(vw))
    return out.reshape(B, G, R, S, dh).transpose(0, 3, 1, 2, 4).reshape(B, S, H * dh)


def _rmsnorm(x, g):
    y = x * lax.rsqrt(jnp.mean(x * x, axis=-1, keepdims=True) + RMS_EPS)
    return y * g


def _rope(x, pos):
    half = ROPE_DIM // 2
    freqs = ROPE_THETA ** (-jnp.arange(half, dtype=jnp.float32) / half)
    ang = pos.astype(jnp.float32)[:, None] * freqs[None, :]
    cos = jnp.cos(ang)[:, None, :]
    sin = jnp.sin(ang)[:, None, :]
    x1, x2, rest = x[..., :half], x[..., half:ROPE_DIM], x[..., ROPE_DIM:]
    return jnp.concatenate([x1 * cos - x2 * sin, x2 * cos + x1 * sin, rest], axis=-1)


def _causal_conv(x, w, b):
    K, C = w.shape
    y = lax.conv_general_dilated(x, w[:, None, :], window_strides=(1,),
                                 padding=[(K - 1, 0)], dimension_numbers=('NWC', 'WIO', 'NWC'),
                                 feature_group_count=C)
    return y + b


def _rg_lru(x, gate_w, gate_b, lam):
    B, S, C = x.shape
    xb = x.reshape(B, S, LRU_BLOCKS, HEAD_DIM)
    gates = jnp.einsum('bsnd,gnde->gbsne', xb, gate_w).reshape(2, B, S, C)
    gates = gates + gate_b[:, None, None, :]
    r = jax.nn.sigmoid(gates[0])
    i = jax.nn.sigmoid(gates[1])
    log_a = -LRU_C * r * jax.nn.softplus(-lam)
    a = jnp.exp(log_a)
    u = jnp.sqrt(-jnp.expm1(2.0 * log_a)) * (i * x)

    def combine(left, right):
        a1, b1 = left
        a2, b2 = right
        return a1 * a2, a2 * b1 + b2

    _, h = lax.associative_scan(combine, (a, u), axis=1)
    return h


def _memory_attention(qm, mem, norm_g, w_kv, q_gain, k_gain):
    B, S, _ = qm.shape
    M = mem.shape[1]
    kv = norm_matmul(mem.reshape(B * M, D_MODEL), norm_g, w_kv).reshape(B, M, 2, MEM_HEADS, HEAD_DIM)
    q = _rmsnorm(qm.reshape(B, S, MEM_HEADS, HEAD_DIM), q_gain)
    k = _rmsnorm(kv[:, :, 0], k_gain)
    v = kv[:, :, 1]
    s = jnp.einsum('bshd,bmhd->bhsm', q, k) * (HEAD_DIM ** -0.5)
    p = jax.nn.softmax(s, axis=-1)
    return jnp.einsum('bhsm,bmhd->bshd', p, v).reshape(B, S, MEM_WIDTH)


def _nsa_shared_kv(x, positions, kv_norm, w_kv_shared, k_gain_shared, cmp_pos, cmp_w1, cmp_b1, cmp_w2):
    B, S, _ = x.shape
    G, dh = NSA_KV_HEADS, HEAD_DIM
    kv = norm_matmul(x.reshape(B * S, D_MODEL), kv_norm, w_kv_shared).reshape(B, S, N_BRANCH, 2, G, dh)
    n_cmp = (S - CMP_LEN) // CMP_STRIDE + 1
    idx = np.arange(n_cmp)[:, None] * CMP_STRIDE + np.arange(CMP_LEN)[None, :]

    def compress(tok, j):
        blk = tok[:, idx] + cmp_pos[j][None, None, :, None, :]
        blk = blk.transpose(0, 1, 3, 2, 4).reshape(B, n_cmp, G, CMP_LEN * dh)
        hid = jax.nn.gelu(blk @ cmp_w1[j] + cmp_b1[j])
        return hid @ cmp_w2[j]

    kc = _rope(_rmsnorm(compress(kv[:, :, 0, 0], 0), k_gain_shared[0]), positions[idx[:, -1]])
    vc = compress(kv[:, :, 0, 1], 1)
    ks = _rope(_rmsnorm(kv[:, :, 1, 0], k_gain_shared[1]), positions)
    vs = kv[:, :, 1, 1]
    kw = _rope(_rmsnorm(kv[:, :, 2, 0], k_gain_shared[2]), positions)
    vw = kv[:, :, 2, 1]
    t = lambda a: a.transpose(0, 2, 1, 3)
    return (t(kc), t(vc), t(ks), t(vs), t(kw), t(vw))


def _peer_residual(x2d, norm_g, w_q, sub_keys, u_tab, v_tab):
    q_all = norm_matmul(x2d, norm_g, w_q)
    e_t, g_t = peer_route(q_all, sub_keys)
    uv = jnp.concatenate([u_tab, v_tab], axis=1)
    return peer_gather(x2d, norm_g, e_t.T, g_t.T, uv)


def kernel(x, mem, positions, norm_mix, norm_ffn, norm_mem, w_out, w_mem_kv, mem_q_gain, mem_k_gain, a_w_in, a_conv_w, a_conv_b, a_gate_w, a_gate_b, a_lambda, b_w_in, b_gate_b, b_q_gain, kv_norm, w_kv_shared, k_gain_shared, cmp_pos, cmp_w1, cmp_b1, cmp_w2, peer_wq, peer_subkeys, peer_u, peer_v):
    B, S, D = x.shape
    T = B * S
    x2d = x.reshape(T, D)
    shared = None
    for l in range(DEPTH):
        if l < N_A:
            proj = norm_matmul(x2d, norm_mix[l], a_w_in[l]).reshape(B, S, -1)
            xb = _causal_conv(proj[..., :LRU_WIDTH], a_conv_w[l], a_conv_b[l])
            yb = proj[..., LRU_WIDTH:2 * LRU_WIDTH]
            qm = proj[..., 2 * LRU_WIDTH:]
            mix = _rg_lru(xb, a_gate_w[l], a_gate_b[l], a_lambda[l]) * jax.nn.gelu(yb)
        else:
            j = l - N_A
            proj = norm_matmul(x2d, norm_mix[l], b_w_in[j]).reshape(B, S, -1)
            q = proj[..., :NSA_WIDTH].reshape(B, S, NSA_HEADS, HEAD_DIM)
            gl = (proj[..., NSA_WIDTH:NSA_WIDTH + N_BRANCH * NSA_HEADS].reshape(B, S, NSA_HEADS, N_BRANCH)
                  + b_gate_b[j].reshape(NSA_HEADS, N_BRANCH))
            qm = proj[..., NSA_WIDTH + N_BRANCH * NSA_HEADS:]
            q = _rope(_rmsnorm(q, b_q_gain[j]), positions)
            mix = nsa_attention(q, gl, *shared)
        mo = _memory_attention(qm, mem, norm_mem[l], w_mem_kv[l], mem_q_gain[l], mem_k_gain[l])
        cat = jnp.concatenate([mix, mo], axis=-1).reshape(T, MIX_WIDTH)
        x2d = matmul_residual(cat, w_out[l], x2d)
        x2d = _peer_residual(x2d, norm_ffn[l], peer_wq[l], peer_subkeys[l], peer_u[l], peer_v[l])
        if l == N_A - 1:
            shared = _nsa_shared_kv(x2d.reshape(B, S, D), positions, kv_norm, w_kv_shared, k_gain_shared,
                                    cmp_pos, cmp_w1, cmp_b1, cmp_w2)
    return x2d.reshape(B, S, D)
```

```python
import jax
import jax.numpy as jnp
import numpy as np
from jax import lax
from jax.experimental import pallas as pl
from jax.experimental.pallas import tpu as pltpu

D_MODEL = 1024
DEPTH = 2
N_A = DEPTH // 2
HEAD_DIM = 64
ROPE_DIM = HEAD_DIM // 4
ROPE_THETA = 500000.0
RMS_EPS = 1e-6
NEG_INF = -1e30
MEM_HEADS = 4
MEM_WIDTH = MEM_HEADS * HEAD_DIM
MIX_WIDTH = D_MODEL
LRU_WIDTH = MIX_WIDTH - MEM_WIDTH
LRU_BLOCKS = LRU_WIDTH // HEAD_DIM
CONV_WIDTH = 4
LRU_C = 8.0
NSA_WIDTH = MIX_WIDTH - MEM_WIDTH
NSA_HEADS = NSA_WIDTH // HEAD_DIM
NSA_KV_HEADS = 2
N_BRANCH = 3
CMP_LEN = 32
CMP_STRIDE = 16
SEL_LEN = 64
SEL_TOP = 16
SEL_FORCE_SCORE = 1e4
WINDOW = 512
Q_CHUNK = 64
N_KEYS = 128
PEER_HEADS = 8
PEER_KEY_DIM = 256
PEER_TOPK = 16
PEER_CHUNK_MAX = 512

VMEM_LIMIT_BYTES = 48 * 1024 * 1024
ROW_TILE = 512


def _norm_matmul_kernel(x_ref, g_ref, w_ref, o_ref):
    x = x_ref[...]
    ms = jnp.mean(x * x, axis=-1, keepdims=True)
    y = x * lax.rsqrt(ms + RMS_EPS) * g_ref[...]
    o_ref[...] = jnp.dot(y.astype(jnp.bfloat16), w_ref[...],
                         preferred_element_type=jnp.float32)


def norm_matmul(x2d, g, w):
    T, D = x2d.shape
    N = w.shape[1]
    return pl.pallas_call(
        _norm_matmul_kernel,
        grid=(T // ROW_TILE,),
        in_specs=[
            pl.BlockSpec((ROW_TILE, D), lambda i: (i, 0)),
            pl.BlockSpec((1, D), lambda i: (0, 0)),
            pl.BlockSpec((D, N), lambda i: (0, 0)),
        ],
        out_specs=pl.BlockSpec((ROW_TILE, N), lambda i: (i, 0)),
        out_shape=jax.ShapeDtypeStruct((T, N), jnp.float32),
        compiler_params=pltpu.CompilerParams(
            dimension_semantics=("arbitrary",), vmem_limit_bytes=VMEM_LIMIT_BYTES),
        name="norm_matmul",
    )(x2d, g.reshape(1, D), w.astype(jnp.bfloat16))


SEQ_TILE = 256
SUBLANES = 8


def _shift_rows(x, k, fill):
    rolled = pltpu.roll(x, k, axis=0)
    row = lax.broadcasted_iota(jnp.int32, x.shape, 0)
    return jnp.where(row >= k, rolled, fill)


def _rg_lru_kernel(p_ref, cw_ref, cb_ref, wg_ref, gb_ref, lam_ref, o_ref, tail_ref, h_ref):
    C = LRU_WIDTH

    @pl.when(pl.program_id(1) == 0)
    def _():
        tail_ref[...] = jnp.zeros_like(tail_ref)
        h_ref[...] = jnp.zeros_like(h_ref)

    xpre = p_ref[:, 0:C]
    yb = p_ref[:, C:2 * C]
    ext = jnp.concatenate([tail_ref[...], xpre], axis=0)
    xb = cb_ref[...] + cw_ref[CONV_WIDTH - 1:CONV_WIDTH, :] * xpre
    for k in range(1, CONV_WIDTH):
        w_k = cw_ref[CONV_WIDTH - 1 - k:CONV_WIDTH - k, :]
        xb = xb + w_k * pltpu.roll(ext, k, axis=0)[SUBLANES:, :]
    tail_ref[...] = xpre[SEQ_TILE - SUBLANES:, :]

    gates = jnp.dot(xb.astype(jnp.bfloat16), wg_ref[...],
                    preferred_element_type=jnp.float32) + gb_ref[...]
    r = jax.nn.sigmoid(gates[:, 0:C])
    ig = jax.nn.sigmoid(gates[:, C:2 * C])
    neg_lam = -lam_ref[...]
    softplus = jnp.maximum(neg_lam, 0.0) + jnp.log1p(jnp.exp(-jnp.abs(neg_lam)))
    log_a = -LRU_C * r * softplus
    a = jnp.exp(log_a)
    u = jnp.sqrt(1.0 - jnp.exp(2.0 * log_a)) * (ig * xb)
    d = 1
    while d < SEQ_TILE:
        u = u + a * _shift_rows(u, d, 0.0)
        a = a * _shift_rows(a, d, 1.0)
        d *= 2
    h = u + a * h_ref[0:1, :]
    h_ref[0:1, :] = h[SEQ_TILE - 1:SEQ_TILE, :]
    o_ref[...] = h * jax.nn.gelu(yb)


def rg_lru_mix(proj, conv_w, conv_b, gate_w, gate_b, lam, batch):
    T = proj.shape[0]
    C = LRU_WIDTH
    S = T // batch
    nt = S // SEQ_TILE
    eye = jnp.eye(LRU_BLOCKS, dtype=gate_w.dtype)
    wg = jnp.einsum('gnde,nm->ndgme', gate_w, eye).reshape(C, 2 * C).astype(jnp.bfloat16)
    full = lambda shape: pl.BlockSpec(shape, lambda b, j: (0, 0))
    return pl.pallas_call(
        _rg_lru_kernel,
        grid=(batch, nt),
        in_specs=[
            pl.BlockSpec((SEQ_TILE, 2 * C), lambda b, j: (b * nt + j, 0)),
            full((CONV_WIDTH, C)), full((1, C)), full((C, 2 * C)), full((1, 2 * C)), full((1, C)),
        ],
        out_specs=pl.BlockSpec((SEQ_TILE, C), lambda b, j: (b * nt + j, 0)),
        out_shape=jax.ShapeDtypeStruct((T, C), jnp.float32),
        scratch_shapes=[pltpu.VMEM((SUBLANES, C), jnp.float32), pltpu.VMEM((SUBLANES, C), jnp.float32)],
        compiler_params=pltpu.CompilerParams(
            dimension_semantics=("arbitrary", "arbitrary"), vmem_limit_bytes=VMEM_LIMIT_BYTES),
        name="rg_lru",
    )(proj, conv_w, conv_b.reshape(1, C), wg, gate_b.reshape(1, 2 * C), lam.reshape(1, C))


def _head_rmsnorm(x, gain):
    return x * lax.rsqrt(jnp.mean(x * x, axis=-1, keepdims=True) + RMS_EPS) * gain


def _mem_out_kernel(mix_ref, qm_ref, kv_ref, qg_ref, kg_ref, w_ref, x_ref, o_ref):
    bf16 = jnp.bfloat16
    acc = x_ref[...] + jnp.dot(mix_ref[...].astype(bf16), w_ref[0:MIX_WIDTH - MEM_WIDTH, :],
                               preferred_element_type=jnp.float32)
    for h in range(MEM_HEADS):
        cols = slice(h * HEAD_DIM, (h + 1) * HEAD_DIM)
        q = _head_rmsnorm(qm_ref[:, cols], qg_ref[...]) * (HEAD_DIM ** -0.5)
        k = _head_rmsnorm(kv_ref[:, cols], kg_ref[...])
        v = kv_ref[:, MEM_WIDTH + h * HEAD_DIM:MEM_WIDTH + (h + 1) * HEAD_DIM]
        s = lax.dot_general(q.astype(bf16), k.astype(bf16), (((1,), (1,)), ((), ())),
                            preferred_element_type=jnp.float32)
        e = jnp.exp(s - jnp.max(s, axis=-1, keepdims=True))
        p = e / jnp.sum(e, axis=-1, keepdims=True)
        o = jnp.dot(p.astype(bf16), v.astype(bf16), preferred_element_type=jnp.float32)
        row0 = MIX_WIDTH - MEM_WIDTH + h * HEAD_DIM
        acc = acc + jnp.dot(o.astype(bf16), w_ref[row0:row0 + HEAD_DIM, :],
                            preferred_element_type=jnp.float32)
    o_ref[...] = acc


def mem_out(mix, qm, kv, q_gain, k_gain, w_out, x2d, batch):
    T, D = x2d.shape
    S = T // batch
    nt = S // SEQ_TILE
    M = kv.shape[0] // batch
    row = lambda width: pl.BlockSpec((SEQ_TILE, width), lambda b, j: (b * nt + j, 0))
    full = lambda shape: pl.BlockSpec(shape, lambda b, j: (0, 0))
    return pl.pallas_call(
        _mem_out_kernel,
        grid=(batch, nt),
        in_specs=[
            row(MIX_WIDTH - MEM_WIDTH), row(MEM_WIDTH),
            pl.BlockSpec((M, 2 * MEM_WIDTH), lambda b, j: (b, 0)),
            full((1, HEAD_DIM)), full((1, HEAD_DIM)), full((MIX_WIDTH, D)), row(D),
        ],
        out_specs=row(D),
        out_shape=jax.ShapeDtypeStruct((T, D), jnp.float32),
        compiler_params=pltpu.CompilerParams(
            dimension_semantics=("arbitrary", "arbitrary"), vmem_limit_bytes=VMEM_LIMIT_BYTES),
        name="mem_out",
    )(mix, qm, kv, q_gain.reshape(1, HEAD_DIM), k_gain.reshape(1, HEAD_DIM),
      w_out.astype(jnp.bfloat16), x2d)


PEER_TOKENS_PER_STEP = 8
PEER_ROWS_PER_TOKEN = PEER_HEADS * PEER_TOPK
PEER_ROWS_PER_STEP = PEER_TOKENS_PER_STEP * PEER_ROWS_PER_TOKEN
PEER_GROUPS_PER_TOKEN = PEER_ROWS_PER_TOKEN // SUBLANES
PEER_GROUPS_PER_STEP = PEER_ROWS_PER_STEP // SUBLANES


LANES = 128
PEER_CHUNKS_PER_ROW = 2 * D_MODEL // LANES
PEER_U_CHUNKS = D_MODEL // LANES


def _peer_gather_kernel(e_hbm, x_ref, gain_ref, gt_ref, uv_hbm, o_ref,
                        idx_smem, buf, idx_sem, row_sem):
    i = pl.program_id(0)
    n = pl.num_programs(0)
    slot = i % 2
    nslot = 1 - slot
    R = PEER_ROWS_PER_STEP
    TB = PEER_TOKENS_PER_STEP

    def idx_copy(step, s):
        return pltpu.make_async_copy(e_hbm.at[pl.ds(step * R, R)],
                                     idx_smem.at[pl.ds(s * R, R)], idx_sem.at[s])

    def row_copy(idx, s, grp, c):
        return pltpu.make_async_copy(uv_hbm.at[idx], buf.at[s, grp, :, c, :], row_sem.at[s])

    def slot_wait(s):
        pltpu.make_async_copy(buf.at[s], buf.at[s], row_sem.at[s]).wait()

    @pl.when(i == 0)
    def _():
        idx_copy(0, 0).start()
        idx_copy(0, 0).wait()
        idx_copy(1, 1).start()

        def body(grp, carry):
            for c in range(SUBLANES):
                row_copy(idx_smem[grp * SUBLANES + c], 0, grp, c).start(priority=c % 2)
            return carry
        lax.fori_loop(0, PEER_GROUPS_PER_STEP, body, 0)

    idx_copy(i + 1, nslot).wait()

    @pl.when(i + 2 <= n)
    def _():
        idx_copy(i + 2, slot).start()

    def issue_token_rows(t):
        base = nslot * R + t * PEER_ROWS_PER_TOKEN
        for gi in range(PEER_GROUPS_PER_TOKEN):
            for c in range(SUBLANES):
                row_copy(idx_smem[base + gi * SUBLANES + c], nslot,
                         t * PEER_GROUPS_PER_TOKEN + gi, c).start(priority=c % 2)

    issue_token_rows(0)
    slot_wait(slot)

    x = x_ref[...]
    ms = jnp.sum(jnp.sum(x * x, axis=2, keepdims=True), axis=1, keepdims=True) * (1.0 / D_MODEL)
    xn = x * lax.rsqrt(ms + RMS_EPS) * gain_ref[...]
    outs = []
    for t in range(TB):
        blk = buf[slot, pl.ds(t * PEER_GROUPS_PER_TOKEN, PEER_GROUPS_PER_TOKEN)]
        u = blk[:, :PEER_U_CHUNKS]
        v = blk[:, PEER_U_CHUNKS:]
        h = jnp.sum(jnp.sum(u * xn[t][None, :, None, :], axis=1), axis=-1, keepdims=True)
        a = jax.nn.gelu(h) * gt_ref[:, t:t + 1].reshape(PEER_GROUPS_PER_TOKEN, SUBLANES, 1)
        o = jnp.sum(jnp.sum(a[:, None, :, :] * v, axis=0), axis=1)
        outs.append(x[t] + o)
        if t + 1 < TB:
            issue_token_rows(t + 1)
    o_ref[...] = jnp.stack(outs, axis=0)

    @pl.when(i == n - 1)
    def _():
        slot_wait(nslot)


def peer_gather(x2d, gain, e, g, uv):
    T, D = x2d.shape
    TB = PEER_TOKENS_PER_STEP
    R = PEER_ROWS_PER_STEP
    n_steps = T // TB
    gt = g.reshape(n_steps, TB, PEER_ROWS_PER_TOKEN).transpose(0, 2, 1)
    uv3 = uv.reshape(uv.shape[0], PEER_CHUNKS_PER_ROW, LANES)
    e_flat = jnp.concatenate([e.reshape(T * PEER_ROWS_PER_TOKEN), jnp.zeros((R,), jnp.int32)])
    out = pl.pallas_call(
        _peer_gather_kernel,
        grid=(n_steps,),
        in_specs=[
            pl.BlockSpec(memory_space=pl.ANY),
            pl.BlockSpec((TB, PEER_U_CHUNKS, LANES), lambda i: (i, 0, 0)),
            pl.BlockSpec((PEER_U_CHUNKS, LANES), lambda i: (0, 0)),
            pl.BlockSpec((None, PEER_ROWS_PER_TOKEN, TB), lambda i: (i, 0, 0)),
            pl.BlockSpec(memory_space=pl.ANY),
        ],
        out_specs=pl.BlockSpec((TB, PEER_U_CHUNKS, LANES), lambda i: (i, 0, 0)),
        out_shape=jax.ShapeDtypeStruct((T, PEER_U_CHUNKS, LANES), jnp.float32),
        scratch_shapes=[
            pltpu.SMEM((2 * R,), jnp.int32),
            pltpu.VMEM((2, PEER_GROUPS_PER_STEP, PEER_CHUNKS_PER_ROW, SUBLANES, LANES), jnp.float32),
            pltpu.SemaphoreType.DMA((2,)),
            pltpu.SemaphoreType.DMA((2,)),
        ],
        compiler_params=pltpu.CompilerParams(
            dimension_semantics=("arbitrary",), vmem_limit_bytes=VMEM_LIMIT_BYTES),
        name="peer_gather",
    )(e_flat, x2d.reshape(T, PEER_U_CHUNKS, LANES), gain.reshape(PEER_U_CHUNKS, LANES), gt, uv3)
    return out.reshape(T, D)


PEER_ROUTE_TOKENS = 256
PEER_HALF = PEER_KEY_DIM // 2


def _top16_rows(s):
    n = s.shape[0]
    row = lax.broadcasted_iota(jnp.int32, s.shape, 0)
    vals, idxs = [], []
    for _ in range(PEER_TOPK):
        m = jnp.max(s, axis=0, keepdims=True)
        idx = jnp.min(jnp.where(s == m, row, n), axis=0, keepdims=True)
        vals.append(m)
        idxs.append(idx)
        s = jnp.where(row == idx, -jnp.inf, s)
    return jnp.concatenate(vals, axis=0), jnp.concatenate(idxs, axis=0)


def _pick_rows(table, which):
    out = jnp.zeros_like(table)
    for a in range(PEER_TOPK):
        out = out + jnp.where(which == a, table[a:a + 1, :], 0)
    return out


def _peer_route_kernel(q_ref, keys_ref, e_ref, g_ref):
    qb = q_ref[...].astype(jnp.bfloat16)
    s1 = lax.dot_general(keys_ref[0], qb[:, :PEER_HALF], (((1,), (1,)), ((), ())),
                         preferred_element_type=jnp.float32)
    s2 = lax.dot_general(keys_ref[1], qb[:, PEER_HALF:], (((1,), (1,)), ((), ())),
                         preferred_element_type=jnp.float32)
    v1, i1 = _top16_rows(s1)
    v2, i2 = _top16_rows(s2)
    cand = jnp.concatenate([v1[a:a + 1, :] + v2 for a in range(PEER_TOPK)], axis=0)
    vals, ci = _top16_rows(cand)
    e_ref[...] = (_pick_rows(i1, lax.shift_right_logical(ci, 4)) * N_KEYS
                  + _pick_rows(i2, ci & (PEER_TOPK - 1)))
    p = jnp.exp(vals - vals[0:1, :])
    g_ref[...] = p / jnp.sum(p, axis=0, keepdims=True)


def peer_route(q_all, sub_keys):
    T = q_all.shape[0]
    TT = PEER_ROUTE_TOKENS
    return pl.pallas_call(
        _peer_route_kernel,
        grid=(T // TT, PEER_HEADS),
        in_specs=[
            pl.BlockSpec((TT, PEER_KEY_DIM), lambda i, h: (i, h)),
            pl.BlockSpec((None, 2, N_KEYS, PEER_HALF), lambda i, h: (h, 0, 0, 0)),
        ],
        out_specs=[
            pl.BlockSpec((PEER_TOPK, TT), lambda i, h: (h, i)),
            pl.BlockSpec((PEER_TOPK, TT), lambda i, h: (h, i)),
        ],
        out_shape=[jax.ShapeDtypeStruct((PEER_ROWS_PER_TOKEN, T), jnp.int32),
                   jax.ShapeDtypeStruct((PEER_ROWS_PER_TOKEN, T), jnp.float32)],
        compiler_params=pltpu.CompilerParams(
            dimension_semantics=("arbitrary", "arbitrary"), vmem_limit_bytes=VMEM_LIMIT_BYTES),
        name="peer_route",
    )(q_all, sub_keys.astype(jnp.bfloat16))


NSA_GROUP = NSA_HEADS // NSA_KV_HEADS
NSA_ROWS = NSA_GROUP * Q_CHUNK
NSA_KEY_TILE = 256


def _qk(qb, k):
    return lax.dot_general(qb, k, (((1,), (1,)), ((), ())), preferred_element_type=jnp.float32)


def _nsa_kernel(q_ref, gl_ref, kc_ref, vc_ref, ov_ref, ks_ref, vs_ref, kw_ref, vw_ref,
                o_ref, mexp_ref):
    c = pl.program_id(1)
    rows = NSA_ROWS
    n_sel = mexp_ref.shape[0] * (NSA_KEY_TILE // SEL_LEN)
    n_cmp_pad = kc_ref.shape[0]
    f32 = jnp.float32
    bf16 = jnp.bfloat16

    qb = (q_ref[...].reshape(rows, HEAD_DIM) * (HEAD_DIM ** -0.5)).astype(bf16)
    t = c * Q_CHUNK + (lax.broadcasted_iota(jnp.int32, (rows, 1), 0) & (Q_CHUNK - 1))

    s = _qk(qb, kc_ref[...])
    cmp_end = lax.broadcasted_iota(jnp.int32, (1, n_cmp_pad), 1) * CMP_STRIDE + (CMP_LEN - 1)
    mask = cmp_end <= t
    m = jnp.max(jnp.where(mask, s, NEG_INF), axis=-1, keepdims=True)
    e = jnp.where(mask, jnp.exp(s - m), 0.0)
    l = jnp.sum(e, axis=-1, keepdims=True)
    pb = (e / jnp.where(l > 0.0, l, 1.0)).astype(bf16)
    o_cmp = jnp.dot(pb, vc_ref[...], preferred_element_type=f32)

    imp = jnp.dot(pb, ov_ref[...], preferred_element_type=f32)
    imp = jnp.sum(imp.reshape(NSA_GROUP, Q_CHUNK, n_sel), axis=0)
    j = lax.broadcasted_iota(jnp.int32, (Q_CHUNK, n_sel), 1)
    forced = (j == 0) | (j == c) | (j == c - 1)
    imp = jnp.where(forced, SEL_FORCE_SCORE, imp)
    imp = jnp.where(j <= c, imp, -1.0)
    rank = jnp.zeros((Q_CHUNK, n_sel), f32)
    for jp in range(n_sel):
        col = imp[:, jp:jp + 1]
        beats = (col > imp) | ((col == imp) & (j > jp))
        rank = rank + jnp.where(beats, 1.0, 0.0)
    sel = jnp.where((rank < float(SEL_TOP)) & (imp >= 0.0), 1.0, 0.0).astype(bf16)
    blk_of_key = lax.broadcasted_iota(jnp.int32, (n_sel, NSA_KEY_TILE), 1) // SEL_LEN
    blk_row = lax.broadcasted_iota(jnp.int32, (n_sel, NSA_KEY_TILE), 0)
    for kt in range(mexp_ref.shape[0]):
        expand = jnp.where(blk_of_key + kt * (NSA_KEY_TILE // SEL_LEN) == blk_row, 1.0, 0.0).astype(bf16)
        mexp_ref[kt] = jnp.dot(sel, expand, preferred_element_type=f32)

    def sel_tile(kt, carry):
        m_i, l_i, acc = carry
        off = pl.multiple_of(kt * NSA_KEY_TILE, NSA_KEY_TILE)
        s = _qk(qb, ks_ref[pl.ds(off, NSA_KEY_TILE), :])
        kpos = off + lax.broadcasted_iota(jnp.int32, (1, NSA_KEY_TILE), 1)
        picked = jnp.concatenate([mexp_ref[kt]] * NSA_GROUP, axis=0) > 0.5
        mask = picked & (kpos <= t)
        m_new = jnp.maximum(m_i, jnp.max(jnp.where(mask, s, NEG_INF), axis=-1, keepdims=True))
        alpha = jnp.exp(m_i - m_new)
        p = jnp.where(mask, jnp.exp(s - m_new), 0.0)
        l_new = alpha * l_i + jnp.sum(p, axis=-1, keepdims=True)
        acc_new = alpha * acc + jnp.dot(p.astype(bf16), vs_ref[pl.ds(off, NSA_KEY_TILE), :],
                                        preferred_element_type=f32)
        return m_new, l_new, acc_new

    n_tiles = c // (NSA_KEY_TILE // SEL_LEN) + 1
    init = (jnp.full((rows, 1), NEG_INF, f32), jnp.zeros((rows, 1), f32),
            jnp.zeros((rows, HEAD_DIM), f32))
    _, l_s, acc_s = lax.fori_loop(0, n_tiles, sel_tile, init)
    o_sel = acc_s / l_s

    start = pl.multiple_of(jnp.maximum(c * Q_CHUNK - WINDOW, 0), Q_CHUNK)
    cur = pl.multiple_of(c * Q_CHUNK, Q_CHUNK)
    s_a = _qk(qb, kw_ref[pl.ds(start, WINDOW), :])
    s_b = _qk(qb, kw_ref[pl.ds(cur, Q_CHUNK), :])
    pos_a = start + lax.broadcasted_iota(jnp.int32, (1, WINDOW), 1)
    pos_b = cur + lax.broadcasted_iota(jnp.int32, (1, Q_CHUNK), 1)
    mask_a = (pos_a < cur) & (pos_a > t - WINDOW)
    mask_b = pos_b <= t
    m_w = jnp.maximum(jnp.max(jnp.where(mask_a, s_a, NEG_INF), axis=-1, keepdims=True),
                      jnp.max(jnp.where(mask_b, s_b, NEG_INF), axis=-1, keepdims=True))
    p_a = jnp.where(mask_a, jnp.exp(s_a - m_w), 0.0)
    p_b = jnp.where(mask_b, jnp.exp(s_b - m_w), 0.0)
    l_w = jnp.sum(p_a, axis=-1, keepdims=True) + jnp.sum(p_b, axis=-1, keepdims=True)
    o_win = (jnp.dot(p_a.astype(bf16), vw_ref[pl.ds(start, WINDOW), :], preferred_element_type=f32)
             + jnp.dot(p_b.astype(bf16), vw_ref[pl.ds(cur, Q_CHUNK), :], preferred_element_type=f32)) / l_w

    g = jax.nn.sigmoid(gl_ref[...])
    o = g[:, 0:1] * o_cmp + g[:, 1:2] * o_sel + g[:, 2:3] * o_win
    o_ref[...] = o.reshape(NSA_GROUP, Q_CHUNK, HEAD_DIM)


def nsa_attention(q, gl, kc, vc, ks, vs, kw, vw):
    B, S, H, dh = q.shape
    G = NSA_KV_HEADS
    R = NSA_GROUP
    NQ = S // Q_CHUNK
    n_sel = S // SEL_LEN
    n_cmp = kc.shape[2]
    n_cmp_pad = -(-n_cmp // 128) * 128
    bf16 = jnp.bfloat16
    qt = q.reshape(B, S, G, R, dh).transpose(0, 2, 3, 1, 4).reshape(B * G, R, S, dh)
    glt = gl.reshape(B, NQ, Q_CHUNK, G, R, N_BRANCH).transpose(0, 3, 1, 4, 2, 5)
    glt = glt.reshape(B * G, NQ, NSA_ROWS, N_BRANCH)
    pad_c = lambda a: jnp.pad(a.reshape(B * G, n_cmp, dh), ((0, 0), (0, n_cmp_pad - n_cmp), (0, 0))).astype(bf16)
    flat = lambda a: a.reshape(B * G, S, dh).astype(bf16)
    c0 = np.arange(n_cmp_pad)[:, None] * CMP_STRIDE
    s0 = np.arange(n_sel)[None, :] * SEL_LEN
    ov = np.clip(np.minimum(c0 + CMP_LEN, s0 + SEL_LEN) - np.maximum(c0, s0), 0, None) / CMP_LEN
    ov[n_cmp:] = 0.0
    kv_spec = pl.BlockSpec((None, S, dh), lambda bg, c: (bg, 0, 0))
    cmp_spec = pl.BlockSpec((None, n_cmp_pad, dh), lambda bg, c: (bg, 0, 0))
    out = pl.pallas_call(
        _nsa_kernel,
        grid=(B * G, NQ),
        in_specs=[
            pl.BlockSpec((None, R, Q_CHUNK, dh), lambda bg, c: (bg, 0, c, 0)),
            pl.BlockSpec((None, None, NSA_ROWS, N_BRANCH), lambda bg, c: (bg, c, 0, 0)),
            cmp_spec, cmp_spec,
            pl.BlockSpec((n_cmp_pad, n_sel), lambda bg, c: (0, 0)),
            kv_spec, kv_spec, kv_spec, kv_spec,
        ],
        out_specs=pl.BlockSpec((None, R, Q_CHUNK, dh), lambda bg, c: (bg, 0, c, 0)),
        out_shape=jax.ShapeDtypeStruct((B * G, R, S, dh), jnp.float32),
        scratch_shapes=[pltpu.VMEM((S // NSA_KEY_TILE, Q_CHUNK, NSA_KEY_TILE), jnp.float32)],
        compiler_params=pltpu.CompilerParams(
            dimension_semantics=("arbitrary", "arbitrary"), vmem_limit_bytes=VMEM_LIMIT_BYTES),
        name="nsa_attention",
    )(qt, glt, pad_c(kc), pad_c(vc), jnp.asarray(ov, bf16), flat(ks), flat(vs), flat(kw), flat(vw))
    return out.reshape(B, G, R, S, dh).transpose(0, 3, 1, 2, 4).reshape(B, S, H * dh)


def _rmsnorm(x, g):
    y = x * lax.rsqrt(jnp.mean(x * x, axis=-1, keepdims=True) + RMS_EPS)
    return y * g


def _rope(x, pos):
    half = ROPE_DIM // 2
    freqs = ROPE_THETA ** (-jnp.arange(half, dtype=jnp.float32) / half)
    ang = pos.astype(jnp.float32)[:, None] * freqs[None, :]
    cos = jnp.cos(ang)[:, None, :]
    sin = jnp.sin(ang)[:, None, :]
    x1, x2, rest = x[..., :half], x[..., half:ROPE_DIM], x[..., ROPE_DIM:]
    return jnp.concatenate([x1 * cos - x2 * sin, x2 * cos + x1 * sin, rest], axis=-1)


def _nsa_shared_kv(x, positions, kv_norm, w_kv_shared, k_gain_shared, cmp_pos, cmp_w1, cmp_b1, cmp_w2):
    B, S, _ = x.shape
    G, dh = NSA_KV_HEADS, HEAD_DIM
    kv = norm_matmul(x.reshape(B * S, D_MODEL), kv_norm, w_kv_shared).reshape(B, S, N_BRANCH, 2, G, dh)
    n_cmp = (S - CMP_LEN) // CMP_STRIDE + 1
    idx = np.arange(n_cmp)[:, None] * CMP_STRIDE + np.arange(CMP_LEN)[None, :]

    def compress(tok, j):
        blk = tok[:, idx] + cmp_pos[j][None, None, :, None, :]
        blk = blk.transpose(0, 1, 3, 2, 4).reshape(B, n_cmp, G, CMP_LEN * dh)
        hid = jax.nn.gelu(blk @ cmp_w1[j] + cmp_b1[j])
        return hid @ cmp_w2[j]

    kc = _rope(_rmsnorm(compress(kv[:, :, 0, 0], 0), k_gain_shared[0]), positions[idx[:, -1]])
    vc = compress(kv[:, :, 0, 1], 1)
    ks = _rope(_rmsnorm(kv[:, :, 1, 0], k_gain_shared[1]), positions)
    vs = kv[:, :, 1, 1]
    kw = _rope(_rmsnorm(kv[:, :, 2, 0], k_gain_shared[2]), positions)
    vw = kv[:, :, 2, 1]
    t = lambda a: a.transpose(0, 2, 1, 3)
    return (t(kc), t(vc), t(ks), t(vs), t(kw), t(vw))


def _peer_residual(x2d, norm_g, w_q, sub_keys, u_tab, v_tab):
    q_all = norm_matmul(x2d, norm_g, w_q)
    e_t, g_t = peer_route(q_all, sub_keys)
    uv = jnp.concatenate([u_tab, v_tab], axis=1)
    return peer_gather(x2d, norm_g, e_t.T, g_t.T, uv)


def kernel(x, mem, positions, norm_mix, norm_ffn, norm_mem, w_out, w_mem_kv, mem_q_gain, mem_k_gain, a_w_in, a_conv_w, a_conv_b, a_gate_w, a_gate_b, a_lambda, b_w_in, b_gate_b, b_q_gain, kv_norm, w_kv_shared, k_gain_shared, cmp_pos, cmp_w1, cmp_b1, cmp_w2, peer_wq, peer_subkeys, peer_u, peer_v):
    B, S, D = x.shape
    T = B * S
    x2d = x.reshape(T, D)
    shared = None
    for l in range(DEPTH):
        if l < N_A:
            proj = norm_matmul(x2d, norm_mix[l], a_w_in[l])
            qm = proj[:, 2 * LRU_WIDTH:]
            mix = rg_lru_mix(proj, a_conv_w[l], a_conv_b[l], a_gate_w[l], a_gate_b[l], a_lambda[l], B)
        else:
            j = l - N_A
            proj = norm_matmul(x2d, norm_mix[l], b_w_in[j])
            qm = proj[:, NSA_WIDTH + N_BRANCH * NSA_HEADS:]
            proj = proj.reshape(B, S, -1)
            q = proj[..., :NSA_WIDTH].reshape(B, S, NSA_HEADS, HEAD_DIM)
            gl = (proj[..., NSA_WIDTH:NSA_WIDTH + N_BRANCH * NSA_HEADS].reshape(B, S, NSA_HEADS, N_BRANCH)
                  + b_gate_b[j].reshape(NSA_HEADS, N_BRANCH))
            q = _rope(_rmsnorm(q, b_q_gain[j]), positions)
            mix = nsa_attention(q, gl, *shared).reshape(T, NSA_WIDTH)
        kv_mem = norm_matmul(mem.reshape(-1, D), norm_mem[l], w_mem_kv[l])
        x2d = mem_out(mix, qm, kv_mem, mem_q_gain[l], mem_k_gain[l], w_out[l], x2d, B)
        x2d = _peer_residual(x2d, norm_ffn[l], peer_wq[l], peer_subkeys[l], peer_u[l], peer_v[l])
        if l == N_A - 1:
            shared = _nsa_shared_kv(x2d.reshape(B, S, D), positions, kv_norm, w_kv_shared, k_gain_shared,
                                    cmp_pos, cmp_w1, cmp_b1, cmp_w2)
    return x2d.reshape(B, S, D)
```

```python
import jax
import jax.numpy as jnp
import numpy as np
from jax import lax
from jax.experimental import pallas as pl
from jax.experimental.pallas import tpu as pltpu

D_MODEL = 1024
DEPTH = 2
N_A = DEPTH // 2
HEAD_DIM = 64
ROPE_DIM = HEAD_DIM // 4
ROPE_THETA = 500000.0
RMS_EPS = 1e-6
NEG_INF = -1e30
MEM_HEADS = 4
MEM_WIDTH = MEM_HEADS * HEAD_DIM
MIX_WIDTH = D_MODEL
LRU_WIDTH = MIX_WIDTH - MEM_WIDTH
LRU_BLOCKS = LRU_WIDTH // HEAD_DIM
CONV_WIDTH = 4
LRU_C = 8.0
NSA_WIDTH = MIX_WIDTH - MEM_WIDTH
NSA_HEADS = NSA_WIDTH // HEAD_DIM
NSA_KV_HEADS = 2
N_BRANCH = 3
CMP_LEN = 32
CMP_STRIDE = 16
SEL_LEN = 64
SEL_TOP = 16
SEL_FORCE_SCORE = 1e4
WINDOW = 512
Q_CHUNK = 64
N_KEYS = 128
PEER_HEADS = 8
PEER_KEY_DIM = 256
PEER_TOPK = 16
PEER_CHUNK_MAX = 512

VMEM_LIMIT_BYTES = 48 * 1024 * 1024
ROW_TILE = 512


def _norm_matmul_kernel(x_ref, g_ref, w_ref, o_ref):
    x = x_ref[...]
    ms = jnp.mean(x * x, axis=-1, keepdims=True)
    y = x * lax.rsqrt(ms + RMS_EPS) * g_ref[...]
    o_ref[...] = jnp.dot(y.astype(jnp.bfloat16), w_ref[...],
                         preferred_element_type=jnp.float32)


def norm_matmul(x2d, g, w):
    T, D = x2d.shape
    N = w.shape[1]
    return pl.pallas_call(
        _norm_matmul_kernel,
        grid=(T // ROW_TILE,),
        in_specs=[
            pl.BlockSpec((ROW_TILE, D), lambda i: (i, 0)),
            pl.BlockSpec((1, D), lambda i: (0, 0)),
            pl.BlockSpec((D, N), lambda i: (0, 0)),
        ],
        out_specs=pl.BlockSpec((ROW_TILE, N), lambda i: (i, 0)),
        out_shape=jax.ShapeDtypeStruct((T, N), jnp.float32),
        compiler_params=pltpu.CompilerParams(
            dimension_semantics=("arbitrary",), vmem_limit_bytes=VMEM_LIMIT_BYTES),
        name="norm_matmul",
    )(x2d, g.reshape(1, D), w.astype(jnp.bfloat16))


SEQ_TILE = 256
SUBLANES = 8


def _shift_rows(x, k, fill):
    rolled = pltpu.roll(x, k, axis=0)
    row = lax.broadcasted_iota(jnp.int32, x.shape, 0)
    return jnp.where(row >= k, rolled, fill)


def _rg_lru_kernel(p_ref, cw_ref, cb_ref, wg_ref, gb_ref, lam_ref, o_ref, tail_ref, h_ref):
    C = LRU_WIDTH

    @pl.when(pl.program_id(1) == 0)
    def _():
        tail_ref[...] = jnp.zeros_like(tail_ref)
        h_ref[...] = jnp.zeros_like(h_ref)

    xpre = p_ref[:, 0:C]
    yb = p_ref[:, C:2 * C]
    ext = jnp.concatenate([tail_ref[...], xpre], axis=0)
    xb = cb_ref[...] + cw_ref[CONV_WIDTH - 1:CONV_WIDTH, :] * xpre
    for k in range(1, CONV_WIDTH):
        w_k = cw_ref[CONV_WIDTH - 1 - k:CONV_WIDTH - k, :]
        xb = xb + w_k * pltpu.roll(ext, k, axis=0)[SUBLANES:, :]
    tail_ref[...] = xpre[SEQ_TILE - SUBLANES:, :]

    gates = jnp.dot(xb.astype(jnp.bfloat16), wg_ref[...],
                    preferred_element_type=jnp.float32) + gb_ref[...]
    r = jax.nn.sigmoid(gates[:, 0:C])
    ig = jax.nn.sigmoid(gates[:, C:2 * C])
    neg_lam = -lam_ref[...]
    softplus = jnp.maximum(neg_lam, 0.0) + jnp.log1p(jnp.exp(-jnp.abs(neg_lam)))
    log_a = -LRU_C * r * softplus
    a = jnp.exp(log_a)
    u = jnp.sqrt(1.0 - jnp.exp(2.0 * log_a)) * (ig * xb)
    d = 1
    while d < SEQ_TILE:
        u = u + a * _shift_rows(u, d, 0.0)
        a = a * _shift_rows(a, d, 1.0)
        d *= 2
    h = u + a * h_ref[0:1, :]
    h_ref[0:1, :] = h[SEQ_TILE - 1:SEQ_TILE, :]
    o_ref[...] = h * jax.nn.gelu(yb)


def rg_lru_mix(proj, conv_w, conv_b, gate_w, gate_b, lam, batch):
    T = proj.shape[0]
    C = LRU_WIDTH
    S = T // batch
    nt = S // SEQ_TILE
    eye = jnp.eye(LRU_BLOCKS, dtype=gate_w.dtype)
    wg = jnp.einsum('gnde,nm->ndgme', gate_w, eye).reshape(C, 2 * C).astype(jnp.bfloat16)
    full = lambda shape: pl.BlockSpec(shape, lambda b, j: (0, 0))
    return pl.pallas_call(
        _rg_lru_kernel,
        grid=(batch, nt),
        in_specs=[
            pl.BlockSpec((SEQ_TILE, 2 * C), lambda b, j: (b * nt + j, 0)),
            full((CONV_WIDTH, C)), full((1, C)), full((C, 2 * C)), full((1, 2 * C)), full((1, C)),
        ],
        out_specs=pl.BlockSpec((SEQ_TILE, C), lambda b, j: (b * nt + j, 0)),
        out_shape=jax.ShapeDtypeStruct((T, C), jnp.float32),
        scratch_shapes=[pltpu.VMEM((SUBLANES, C), jnp.float32), pltpu.VMEM((SUBLANES, C), jnp.float32)],
        compiler_params=pltpu.CompilerParams(
            dimension_semantics=("arbitrary", "arbitrary"), vmem_limit_bytes=VMEM_LIMIT_BYTES),
        name="rg_lru",
    )(proj, conv_w, conv_b.reshape(1, C), wg, gate_b.reshape(1, 2 * C), lam.reshape(1, C))


def _head_rmsnorm(x, gain):
    return x * lax.rsqrt(jnp.mean(x * x, axis=-1, keepdims=True) + RMS_EPS) * gain


def _mem_out_kernel(mix_ref, qm_ref, kv_ref, qg_ref, kg_ref, w_ref, x_ref, o_ref):
    bf16 = jnp.bfloat16
    acc = x_ref[...] + jnp.dot(mix_ref[...].astype(bf16), w_ref[0:MIX_WIDTH - MEM_WIDTH, :],
                               preferred_element_type=jnp.float32)
    for h in range(MEM_HEADS):
        cols = slice(h * HEAD_DIM, (h + 1) * HEAD_DIM)
        q = _head_rmsnorm(qm_ref[:, cols], qg_ref[...]) * (HEAD_DIM ** -0.5)
        k = _head_rmsnorm(kv_ref[:, cols], kg_ref[...])
        v = kv_ref[:, MEM_WIDTH + h * HEAD_DIM:MEM_WIDTH + (h + 1) * HEAD_DIM]
        s = lax.dot_general(q.astype(bf16), k.astype(bf16), (((1,), (1,)), ((), ())),
                            preferred_element_type=jnp.float32)
        e = jnp.exp(s - jnp.max(s, axis=-1, keepdims=True))
        p = e / jnp.sum(e, axis=-1, keepdims=True)
        o = jnp.dot(p.astype(bf16), v.astype(bf16), preferred_element_type=jnp.float32)
        row0 = MIX_WIDTH - MEM_WIDTH + h * HEAD_DIM
        acc = acc + jnp.dot(o.astype(bf16), w_ref[row0:row0 + HEAD_DIM, :],
                            preferred_element_type=jnp.float32)
    o_ref[...] = acc


def mem_out(mix, qm, kv, q_gain, k_gain, w_out, x2d, batch):
    T, D = x2d.shape
    S = T // batch
    nt = S // SEQ_TILE
    M = kv.shape[0] // batch
    row = lambda width: pl.BlockSpec((SEQ_TILE, width), lambda b, j: (b * nt + j, 0))
    full = lambda shape: pl.BlockSpec(shape, lambda b, j: (0, 0))
    return pl.pallas_call(
        _mem_out_kernel,
        grid=(batch, nt),
        in_specs=[
            row(MIX_WIDTH - MEM_WIDTH), row(MEM_WIDTH),
            pl.BlockSpec((M, 2 * MEM_WIDTH), lambda b, j: (b, 0)),
            full((1, HEAD_DIM)), full((1, HEAD_DIM)), full((MIX_WIDTH, D)), row(D),
        ],
        out_specs=row(D),
        out_shape=jax.ShapeDtypeStruct((T, D), jnp.float32),
        compiler_params=pltpu.CompilerParams(
            dimension_semantics=("arbitrary", "arbitrary"), vmem_limit_bytes=VMEM_LIMIT_BYTES),
        name="mem_out",
    )(mix, qm, kv, q_gain.reshape(1, HEAD_DIM), k_gain.reshape(1, HEAD_DIM),
      w_out.astype(jnp.bfloat16), x2d)


PEER_TOKENS_PER_STEP = 8
PEER_ROWS_PER_TOKEN = PEER_HEADS * PEER_TOPK
PEER_ROWS_PER_STEP = PEER_TOKENS_PER_STEP * PEER_ROWS_PER_TOKEN
PEER_GROUPS_PER_TOKEN = PEER_ROWS_PER_TOKEN // SUBLANES
PEER_GROUPS_PER_STEP = PEER_ROWS_PER_STEP // SUBLANES


LANES = 128
PEER_CHUNKS_PER_ROW = 2 * D_MODEL // LANES
PEER_U_CHUNKS = D_MODEL // LANES


PEER_SLOTS = 3


def _peer_gather_kernel(e_hbm, x_ref, gain_ref, gt_ref, uv_hbm, o_ref,
                        idx_smem, buf, idx_sem, row_sem):
    i = pl.program_id(0)
    n = pl.num_programs(0)
    slot = lax.rem(i, PEER_SLOTS)
    slot1 = lax.rem(i + 1, PEER_SLOTS)
    slot2 = lax.rem(i + 2, PEER_SLOTS)
    R = PEER_ROWS_PER_STEP
    TB = PEER_TOKENS_PER_STEP

    def idx_copy(step, s):
        return pltpu.make_async_copy(e_hbm.at[pl.ds(step * R, R)],
                                     idx_smem.at[pl.ds(s * R, R)], idx_sem.at[s])

    def row_copy(idx, s, grp, c):
        return pltpu.make_async_copy(uv_hbm.at[idx], buf.at[s, grp, :, c, :], row_sem.at[s])

    def slot_wait(s):
        pltpu.make_async_copy(buf.at[s], buf.at[s], row_sem.at[s]).wait()

    @pl.when(i == 0)
    def _():
        idx_copy(0, 0).start()
        idx_copy(1, 1).start()
        idx_copy(0, 0).wait()
        idx_copy(1, 1).wait()
        idx_copy(2, 2).start()

        def body(grp, carry):
            for c in range(SUBLANES):
                row_copy(idx_smem[grp * SUBLANES + c], grp // PEER_GROUPS_PER_STEP,
                         lax.rem(grp, PEER_GROUPS_PER_STEP), c).start(priority=c % 2)
            return carry
        lax.fori_loop(0, 2 * PEER_GROUPS_PER_STEP, body, 0)

    idx_copy(i + 2, slot2).wait()

    @pl.when(i + 3 <= n + 1)
    def _():
        idx_copy(i + 3, slot).start()

    def issue_token_rows(t):
        base = slot2 * R + t * PEER_ROWS_PER_TOKEN
        for gi in range(PEER_GROUPS_PER_TOKEN):
            for c in range(SUBLANES):
                row_copy(idx_smem[base + gi * SUBLANES + c], slot2,
                         t * PEER_GROUPS_PER_TOKEN + gi, c).start(priority=c % 2)

    issue_token_rows(0)
    slot_wait(slot)

    x = x_ref[...]
    ms = jnp.sum(jnp.sum(x * x, axis=2, keepdims=True), axis=1, keepdims=True) * (1.0 / D_MODEL)
    xn = x * lax.rsqrt(ms + RMS_EPS) * gain_ref[...]
    outs = []
    for t in range(TB):
        blk = buf[slot, pl.ds(t * PEER_GROUPS_PER_TOKEN, PEER_GROUPS_PER_TOKEN)]
        u = blk[:, :PEER_U_CHUNKS]
        v = blk[:, PEER_U_CHUNKS:]
        h = jnp.sum(jnp.sum(u * xn[t][None, :, None, :], axis=1), axis=-1, keepdims=True)
        a = jax.nn.gelu(h) * gt_ref[:, t:t + 1].reshape(PEER_GROUPS_PER_TOKEN, SUBLANES, 1)
        o = jnp.sum(jnp.sum(a[:, None, :, :] * v, axis=0), axis=1)
        outs.append(x[t] + o)
        if t + 1 < TB:
            issue_token_rows(t + 1)
    o_ref[...] = jnp.stack(outs, axis=0)

    @pl.when(i == n - 1)
    def _():
        slot_wait(slot1)
        slot_wait(slot2)


def peer_gather(x2d, gain, e, g, uv):
    T, D = x2d.shape
    TB = PEER_TOKENS_PER_STEP
    R = PEER_ROWS_PER_STEP
    n_steps = T // TB
    gt = g.reshape(n_steps, TB, PEER_ROWS_PER_TOKEN).transpose(0, 2, 1)
    uv3 = uv.reshape(uv.shape[0], PEER_CHUNKS_PER_ROW, LANES)
    e_flat = jnp.concatenate([e.reshape(T * PEER_ROWS_PER_TOKEN),
                              jnp.zeros(((PEER_SLOTS - 1) * R,), jnp.int32)])
    out = pl.pallas_call(
        _peer_gather_kernel,
        grid=(n_steps,),
        in_specs=[
            pl.BlockSpec(memory_space=pl.ANY),
            pl.BlockSpec((TB, PEER_U_CHUNKS, LANES), lambda i: (i, 0, 0)),
            pl.BlockSpec((PEER_U_CHUNKS, LANES), lambda i: (0, 0)),
            pl.BlockSpec((None, PEER_ROWS_PER_TOKEN, TB), lambda i: (i, 0, 0)),
            pl.BlockSpec(memory_space=pl.ANY),
        ],
        out_specs=pl.BlockSpec((TB, PEER_U_CHUNKS, LANES), lambda i: (i, 0, 0)),
        out_shape=jax.ShapeDtypeStruct((T, PEER_U_CHUNKS, LANES), jnp.float32),
        scratch_shapes=[
            pltpu.SMEM((PEER_SLOTS * R,), jnp.int32),
            pltpu.VMEM((PEER_SLOTS, PEER_GROUPS_PER_STEP, PEER_CHUNKS_PER_ROW, SUBLANES, LANES), jnp.float32),
            pltpu.SemaphoreType.DMA((PEER_SLOTS,)),
            pltpu.SemaphoreType.DMA((PEER_SLOTS,)),
        ],
        compiler_params=pltpu.CompilerParams(
            dimension_semantics=("arbitrary",), vmem_limit_bytes=VMEM_LIMIT_BYTES),
        name="peer_gather",
    )(e_flat, x2d.reshape(T, PEER_U_CHUNKS, LANES), gain.reshape(PEER_U_CHUNKS, LANES), gt, uv3)
    return out.reshape(T, D)


PEER_ROUTE_TOKENS = 256
PEER_HALF = PEER_KEY_DIM // 2


def _top16_rows(s, ids=None):
    if ids is None:
        ids = lax.broadcasted_iota(jnp.int32, s.shape, 0)
    vals, idxs = [], []
    for _ in range(PEER_TOPK):
        m = jnp.max(s, axis=0, keepdims=True)
        idx = jnp.min(jnp.where(s == m, ids, jnp.iinfo(jnp.int32).max), axis=0, keepdims=True)
        vals.append(m)
        idxs.append(idx)
        s = jnp.where(ids == idx, -jnp.inf, s)
    return jnp.concatenate(vals, axis=0), jnp.concatenate(idxs, axis=0)


def _pair_candidates(v1, v2):
    n = PEER_TOPK
    tokens = v1.shape[1]
    sums, ids = [], []

    def add(piece, a_of_row, b_of_row):
        ok = (a_of_row + 1) * (b_of_row + 1) <= n
        sums.append(jnp.where(ok, piece, -jnp.inf))
        ids.append(a_of_row * n + b_of_row)

    for a in range(4):
        rows = n if a == 0 else SUBLANES
        b_row = lax.broadcasted_iota(jnp.int32, (rows, tokens), 0)
        add(v1[a:a + 1, :] + v2[0:rows, :], jnp.full_like(b_row, a), b_row)
    for b in range(3):
        rows = n if b == 0 else SUBLANES
        a_row = lax.broadcasted_iota(jnp.int32, (rows, tokens), 0)
        piece = jnp.where(a_row >= 4, v2[b:b + 1, :] + v1[0:rows, :], -jnp.inf)
        add(piece, a_row, jnp.full_like(a_row, b))
    return jnp.concatenate(sums, axis=0), jnp.concatenate(ids, axis=0)


def _pick_rows(table, which):
    out = jnp.zeros_like(table)
    for a in range(PEER_TOPK):
        out = out + jnp.where(which == a, table[a:a + 1, :], 0)
    return out


def _peer_route_kernel(q_ref, keys_ref, e_ref, g_ref):
    qb = q_ref[...].astype(jnp.bfloat16)
    s1 = lax.dot_general(keys_ref[0], qb[:, :PEER_HALF], (((1,), (1,)), ((), ())),
                         preferred_element_type=jnp.float32)
    s2 = lax.dot_general(keys_ref[1], qb[:, PEER_HALF:], (((1,), (1,)), ((), ())),
                         preferred_element_type=jnp.float32)
    v1, i1 = _top16_rows(s1)
    v2, i2 = _top16_rows(s2)
    vals, ci = _top16_rows(*_pair_candidates(v1, v2))
    e_ref[...] = (_pick_rows(i1, lax.shift_right_logical(ci, 4)) * N_KEYS
                  + _pick_rows(i2, ci & (PEER_TOPK - 1)))
    p = jnp.exp(vals - vals[0:1, :])
    g_ref[...] = p / jnp.sum(p, axis=0, keepdims=True)


def peer_route(q_all, sub_keys):
    T = q_all.shape[0]
    TT = PEER_ROUTE_TOKENS
    return pl.pallas_call(
        _peer_route_kernel,
        grid=(T // TT, PEER_HEADS),
        in_specs=[
            pl.BlockSpec((TT, PEER_KEY_DIM), lambda i, h: (i, h)),
            pl.BlockSpec((None, 2, N_KEYS, PEER_HALF), lambda i, h: (h, 0, 0, 0)),
        ],
        out_specs=[
            pl.BlockSpec((PEER_TOPK, TT), lambda i, h: (h, i)),
            pl.BlockSpec((PEER_TOPK, TT), lambda i, h: (h, i)),
        ],
        out_shape=[jax.ShapeDtypeStruct((PEER_ROWS_PER_TOKEN, T), jnp.int32),
                   jax.ShapeDtypeStruct((PEER_ROWS_PER_TOKEN, T), jnp.float32)],
        compiler_params=pltpu.CompilerParams(
            dimension_semantics=("arbitrary", "arbitrary"), vmem_limit_bytes=VMEM_LIMIT_BYTES),
        name="peer_route",
    )(q_all, sub_keys.astype(jnp.bfloat16))


NSA_GROUP = NSA_HEADS // NSA_KV_HEADS
NSA_ROWS = NSA_GROUP * Q_CHUNK
NSA_KEY_TILE = 512


def _qk(qb, k):
    return lax.dot_general(qb, k, (((1,), (1,)), ((), ())), preferred_element_type=jnp.float32)


def _nsa_kernel(q_ref, gl_ref, kc_ref, vc_ref, ov_ref, ks_ref, vs_ref, kw_ref, vw_ref,
                o_ref, mexp_ref):
    c = pl.program_id(1)
    rows = NSA_ROWS
    n_sel = mexp_ref.shape[0] * (NSA_KEY_TILE // SEL_LEN)
    n_cmp_pad = kc_ref.shape[0]
    f32 = jnp.float32
    bf16 = jnp.bfloat16

    qb = (q_ref[...].reshape(rows, HEAD_DIM) * (HEAD_DIM ** -0.5)).astype(bf16)
    t = c * Q_CHUNK + (lax.broadcasted_iota(jnp.int32, (rows, 1), 0) & (Q_CHUNK - 1))

    s = _qk(qb, kc_ref[...])
    cmp_end = lax.broadcasted_iota(jnp.int32, (1, n_cmp_pad), 1) * CMP_STRIDE + (CMP_LEN - 1)
    mask = cmp_end <= t
    m = jnp.max(jnp.where(mask, s, NEG_INF), axis=-1, keepdims=True)
    e = jnp.where(mask, jnp.exp(s - m), 0.0)
    l = jnp.sum(e, axis=-1, keepdims=True)
    pb = (e / jnp.where(l > 0.0, l, 1.0)).astype(bf16)
    o_cmp = jnp.dot(pb, vc_ref[...], preferred_element_type=f32)

    imp = jnp.dot(pb, ov_ref[...], preferred_element_type=f32)
    imp = jnp.sum(imp.reshape(NSA_GROUP, Q_CHUNK, n_sel), axis=0)
    j = lax.broadcasted_iota(jnp.int32, (Q_CHUNK, n_sel), 1)
    forced = (j == 0) | (j == c) | (j == c - 1)
    imp = jnp.where(forced, SEL_FORCE_SCORE, imp)
    imp = jnp.where(j <= c, imp, -1.0)
    rank = jnp.zeros((Q_CHUNK, n_sel), f32)
    for jp in range(n_sel):
        col = imp[:, jp:jp + 1]
        beats = (col > imp) | ((col == imp) & (j > jp))
        rank = rank + jnp.where(beats, 1.0, 0.0)
    sel = jnp.where((rank < float(SEL_TOP)) & (imp >= 0.0), 1.0, 0.0).astype(bf16)
    blk_of_key = lax.broadcasted_iota(jnp.int32, (n_sel, NSA_KEY_TILE), 1) // SEL_LEN
    blk_row = lax.broadcasted_iota(jnp.int32, (n_sel, NSA_KEY_TILE), 0)
    for kt in range(mexp_ref.shape[0]):
        expand = jnp.where(blk_of_key + kt * (NSA_KEY_TILE // SEL_LEN) == blk_row, 1.0, 0.0).astype(bf16)
        mexp_ref[kt] = jnp.dot(sel, expand, preferred_element_type=f32)

    def sel_tile(kt, carry):
        m_i, l_i, acc = carry
        off = pl.multiple_of(kt * NSA_KEY_TILE, NSA_KEY_TILE)
        s = _qk(qb, ks_ref[pl.ds(off, NSA_KEY_TILE), :])
        kpos = off + lax.broadcasted_iota(jnp.int32, (1, NSA_KEY_TILE), 1)
        picked = jnp.concatenate([mexp_ref[kt]] * NSA_GROUP, axis=0) > 0.5
        mask = picked & (kpos <= t)
        m_new = jnp.maximum(m_i, jnp.max(jnp.where(mask, s, NEG_INF), axis=-1, keepdims=True))
        alpha = jnp.exp(m_i - m_new)
        p = jnp.where(mask, jnp.exp(s - m_new), 0.0)
        l_new = alpha * l_i + jnp.sum(p, axis=-1, keepdims=True)
        acc_new = alpha * acc + jnp.dot(p.astype(bf16), vs_ref[pl.ds(off, NSA_KEY_TILE), :],
                                        preferred_element_type=f32)
        return m_new, l_new, acc_new

    n_tiles = c // (NSA_KEY_TILE // SEL_LEN) + 1
    init = (jnp.full((rows, 1), NEG_INF, f32), jnp.zeros((rows, 1), f32),
            jnp.zeros((rows, HEAD_DIM), f32))
    _, l_s, acc_s = lax.fori_loop(0, n_tiles, sel_tile, init)
    o_sel = acc_s / l_s

    start = pl.multiple_of(jnp.maximum(c * Q_CHUNK - WINDOW, 0), Q_CHUNK)
    cur = pl.multiple_of(c * Q_CHUNK, Q_CHUNK)
    s_a = _qk(qb, kw_ref[pl.ds(start, WINDOW), :])
    s_b = _qk(qb, kw_ref[pl.ds(cur, Q_CHUNK), :])
    pos_a = start + lax.broadcasted_iota(jnp.int32, (1, WINDOW), 1)
    pos_b = cur + lax.broadcasted_iota(jnp.int32, (1, Q_CHUNK), 1)
    mask_a = (pos_a < cur) & (pos_a > t - WINDOW)
    mask_b = pos_b <= t
    m_w = jnp.maximum(jnp.max(jnp.where(mask_a, s_a, NEG_INF), axis=-1, keepdims=True),
                      jnp.max(jnp.where(mask_b, s_b, NEG_INF), axis=-1, keepdims=True))
    p_a = jnp.where(mask_a, jnp.exp(s_a - m_w), 0.0)
    p_b = jnp.where(mask_b, jnp.exp(s_b - m_w), 0.0)
    l_w = jnp.sum(p_a, axis=-1, keepdims=True) + jnp.sum(p_b, axis=-1, keepdims=True)
    o_win = (jnp.dot(p_a.astype(bf16), vw_ref[pl.ds(start, WINDOW), :], preferred_element_type=f32)
             + jnp.dot(p_b.astype(bf16), vw_ref[pl.ds(cur, Q_CHUNK), :], preferred_element_type=f32)) / l_w

    g = jax.nn.sigmoid(gl_ref[...])
    o = g[:, 0:1] * o_cmp + g[:, 1:2] * o_sel + g[:, 2:3] * o_win
    o_ref[...] = o.reshape(NSA_GROUP, Q_CHUNK, HEAD_DIM)


def nsa_attention(q, gl, kc, vc, ks, vs, kw, vw):
    B, S, H, dh = q.shape
    G = NSA_KV_HEADS
    R = NSA_GROUP
    NQ = S // Q_CHUNK
    n_sel = S // SEL_LEN
    n_cmp = kc.shape[2]
    n_cmp_pad = -(-n_cmp // 128) * 128
    bf16 = jnp.bfloat16
    qt = q.reshape(B, S, G, R, dh).transpose(0, 2, 3, 1, 4).reshape(B * G, R, S, dh)
    glt = gl.reshape(B, NQ, Q_CHUNK, G, R, N_BRANCH).transpose(0, 3, 1, 4, 2, 5)
    glt = glt.reshape(B * G, NQ, NSA_ROWS, N_BRANCH)
    pad_c = lambda a: jnp.pad(a.reshape(B * G, n_cmp, dh), ((0, 0), (0, n_cmp_pad - n_cmp), (0, 0))).astype(bf16)
    flat = lambda a: a.reshape(B * G, S, dh).astype(bf16)
    c0 = np.arange(n_cmp_pad)[:, None] * CMP_STRIDE
    s0 = np.arange(n_sel)[None, :] * SEL_LEN
    ov = np.clip(np.minimum(c0 + CMP_LEN, s0 + SEL_LEN) - np.maximum(c0, s0), 0, None) / CMP_LEN
    ov[n_cmp:] = 0.0
    kv_spec = pl.BlockSpec((None, S, dh), lambda bg, c: (bg, 0, 0))
    cmp_spec = pl.BlockSpec((None, n_cmp_pad, dh), lambda bg, c: (bg, 0, 0))
    out = pl.pallas_call(
        _nsa_kernel,
        grid=(B * G, NQ),
        in_specs=[
            pl.BlockSpec((None, R, Q_CHUNK, dh), lambda bg, c: (bg, 0, c, 0)),
            pl.BlockSpec((None, None, NSA_ROWS, N_BRANCH), lambda bg, c: (bg, c, 0, 0)),
            cmp_spec, cmp_spec,
            pl.BlockSpec((n_cmp_pad, n_sel), lambda bg, c: (0, 0)),
            kv_spec, kv_spec, kv_spec, kv_spec,
        ],
        out_specs=pl.BlockSpec((None, R, Q_CHUNK, dh), lambda bg, c: (bg, 0, c, 0)),
        out_shape=jax.ShapeDtypeStruct((B * G, R, S, dh), jnp.float32),
        scratch_shapes=[pltpu.VMEM((S // NSA_KEY_TILE, Q_CHUNK, NSA_KEY_TILE), jnp.float32)],
        compiler_params=pltpu.CompilerParams(
            dimension_semantics=("arbitrary", "arbitrary"), vmem_limit_bytes=VMEM_LIMIT_BYTES),
        name="nsa_attention",
    )(qt, glt, pad_c(kc), pad_c(vc), jnp.asarray(ov, bf16), flat(ks), flat(vs), flat(kw), flat(vw))
    return out.reshape(B, G, R, S, dh).transpose(0, 3, 1, 2, 4).reshape(B, S, H * dh)


def _rmsnorm(x, g):
    y = x * lax.rsqrt(jnp.mean(x * x, axis=-1, keepdims=True) + RMS_EPS)
    return y * g


def _rope(x, pos):
    half = ROPE_DIM // 2
    freqs = ROPE_THETA ** (-jnp.arange(half, dtype=jnp.float32) / half)
    ang = pos.astype(jnp.float32)[:, None] * freqs[None, :]
    cos = jnp.cos(ang)[:, None, :]
    sin = jnp.sin(ang)[:, None, :]
    x1, x2, rest = x[..., :half], x[..., half:ROPE_DIM], x[..., ROPE_DIM:]
    return jnp.concatenate([x1 * cos - x2 * sin, x2 * cos + x1 * sin, rest], axis=-1)


def _nsa_shared_kv(x, positions, kv_norm, w_kv_shared, k_gain_shared, cmp_pos, cmp_w1, cmp_b1, cmp_w2):
    B, S, _ = x.shape
    G, dh = NSA_KV_HEADS, HEAD_DIM
    kv = norm_matmul(x.reshape(B * S, D_MODEL), kv_norm, w_kv_shared).reshape(B, S, N_BRANCH, 2, G, dh)
    n_cmp = (S - CMP_LEN) // CMP_STRIDE + 1
    idx = np.arange(n_cmp)[:, None] * CMP_STRIDE + np.arange(CMP_LEN)[None, :]

    def compress(tok, j):
        blk = tok[:, idx] + cmp_pos[j][None, None, :, None, :]
        blk = blk.transpose(0, 1, 3, 2, 4).reshape(B, n_cmp, G, CMP_LEN * dh)
        hid = jax.nn.gelu(blk @ cmp_w1[j] + cmp_b1[j])
        return hid @ cmp_w2[j]

    kc = _rope(_rmsnorm(compress(kv[:, :, 0, 0], 0), k_gain_shared[0]), positions[idx[:, -1]])
    vc = compress(kv[:, :, 0, 1], 1)
    ks = _rope(_rmsnorm(kv[:, :, 1, 0], k_gain_shared[1]), positions)
    vs = kv[:, :, 1, 1]
    kw = _rope(_rmsnorm(kv[:, :, 2, 0], k_gain_shared[2]), positions)
    vw = kv[:, :, 2, 1]
    t = lambda a: a.transpose(0, 2, 1, 3)
    return (t(kc), t(vc), t(ks), t(vs), t(kw), t(vw))


def _peer_residual(x2d, norm_g, w_q, sub_keys, u_tab, v_tab):
    q_all = norm_matmul(x2d, norm_g, w_q)
    e_t, g_t = peer_route(q_all, sub_keys)
    uv = jnp.concatenate([u_tab, v_tab], axis=1)
    return peer_gather(x2d, norm_g, e_t.T, g_t.T, uv)


def kernel(x, mem, positions, norm_mix, norm_ffn, norm_mem, w_out, w_mem_kv, mem_q_gain, mem_k_gain, a_w_in, a_conv_w, a_conv_b, a_gate_w, a_gate_b, a_lambda, b_w_in, b_gate_b, b_q_gain, kv_norm, w_kv_shared, k_gain_shared, cmp_pos, cmp_w1, cmp_b1, cmp_w2, peer_wq, peer_subkeys, peer_u, peer_v):
    B, S, D = x.shape
    T = B * S
    x2d = x.reshape(T, D)
    shared = None
    for l in range(DEPTH):
        if l < N_A:
            proj = norm_matmul(x2d, norm_mix[l], a_w_in[l])
            qm = proj[:, 2 * LRU_WIDTH:]
            mix = rg_lru_mix(proj, a_conv_w[l], a_conv_b[l], a_gate_w[l], a_gate_b[l], a_lambda[l], B)
        else:
            j = l - N_A
            proj = norm_matmul(x2d, norm_mix[l], b_w_in[j])
            qm = proj[:, NSA_WIDTH + N_BRANCH * NSA_HEADS:]
            proj = proj.reshape(B, S, -1)
            q = proj[..., :NSA_WIDTH].reshape(B, S, NSA_HEADS, HEAD_DIM)
            gl = (proj[..., NSA_WIDTH:NSA_WIDTH + N_BRANCH * NSA_HEADS].reshape(B, S, NSA_HEADS, N_BRANCH)
                  + b_gate_b[j].reshape(NSA_HEADS, N_BRANCH))
            q = _rope(_rmsnorm(q, b_q_gain[j]), positions)
            mix = nsa_attention(q, gl, *shared).reshape(T, NSA_WIDTH)
        kv_mem = norm_matmul(mem.reshape(-1, D), norm_mem[l], w_mem_kv[l])
        x2d = mem_out(mix, qm, kv_mem, mem_q_gain[l], mem_k_gain[l], w_out[l], x2d, B)
        x2d = _peer_residual(x2d, norm_ffn[l], peer_wq[l], peer_subkeys[l], peer_u[l], peer_v[l])
        if l == N_A - 1:
            shared = _nsa_shared_kv(x2d.reshape(B, S, D), positions, kv_norm, w_kv_shared, k_gain_shared,
                                    cmp_pos, cmp_w1, cmp_b1, cmp_w2)
    return x2d.reshape(B, S, D)
```

```python
import jax
import jax.numpy as jnp
import numpy as np
from jax import lax
from jax.experimental import pallas as pl
from jax.experimental.pallas import tpu as pltpu

D_MODEL = 1024
DEPTH = 2
N_A = DEPTH // 2
HEAD_DIM = 64
ROPE_DIM = HEAD_DIM // 4
ROPE_THETA = 500000.0
RMS_EPS = 1e-6
NEG_INF = -1e30
MASK_FLOOR = -1e29
MEM_HEADS = 4
MEM_WIDTH = MEM_HEADS * HEAD_DIM
MIX_WIDTH = D_MODEL
LRU_WIDTH = MIX_WIDTH - MEM_WIDTH
LRU_BLOCKS = LRU_WIDTH // HEAD_DIM
CONV_WIDTH = 4
LRU_C = 8.0
NSA_WIDTH = MIX_WIDTH - MEM_WIDTH
NSA_HEADS = NSA_WIDTH // HEAD_DIM
NSA_KV_HEADS = 2
N_BRANCH = 3
CMP_LEN = 32
CMP_STRIDE = 16
SEL_LEN = 64
SEL_TOP = 16
SEL_FORCE_SCORE = 1e4
WINDOW = 512
Q_CHUNK = 64
N_KEYS = 128
PEER_HEADS = 8
PEER_KEY_DIM = 256
PEER_TOPK = 16
PEER_CHUNK_MAX = 512

VMEM_LIMIT_BYTES = 48 * 1024 * 1024
ROW_TILE = 512


def _norm_matmul_kernel(x_ref, g_ref, w_ref, o_ref):
    x = x_ref[...]
    ms = jnp.mean(x * x, axis=-1, keepdims=True)
    y = x * lax.rsqrt(ms + RMS_EPS) * g_ref[...]
    o_ref[...] = jnp.dot(y.astype(jnp.bfloat16), w_ref[...],
                         preferred_element_type=jnp.float32)


def norm_matmul(x2d, g, w):
    T, D = x2d.shape
    N = w.shape[1]
    return pl.pallas_call(
        _norm_matmul_kernel,
        grid=(T // ROW_TILE,),
        in_specs=[
            pl.BlockSpec((ROW_TILE, D), lambda i: (i, 0)),
            pl.BlockSpec((1, D), lambda i: (0, 0)),
            pl.BlockSpec((D, N), lambda i: (0, 0)),
        ],
        out_specs=pl.BlockSpec((ROW_TILE, N), lambda i: (i, 0)),
        out_shape=jax.ShapeDtypeStruct((T, N), jnp.float32),
        compiler_params=pltpu.CompilerParams(
            dimension_semantics=("arbitrary",), vmem_limit_bytes=VMEM_LIMIT_BYTES),
        name="norm_matmul",
    )(x2d, g.reshape(1, D), w.astype(jnp.bfloat16))


SEQ_TILE = 256
SUBLANES = 8


def _shift_rows(x, k, fill):
    rolled = pltpu.roll(x, k, axis=0)
    row = lax.broadcasted_iota(jnp.int32, x.shape, 0)
    return jnp.where(row >= k, rolled, fill)


def _rg_lru_kernel(p_ref, cw_ref, cb_ref, wg_ref, gb_ref, lam_ref, o_ref, tail_ref, h_ref):
    C = LRU_WIDTH

    @pl.when(pl.program_id(1) == 0)
    def _():
        tail_ref[...] = jnp.zeros_like(tail_ref)
        h_ref[...] = jnp.zeros_like(h_ref)

    xpre = p_ref[:, 0:C]
    yb = p_ref[:, C:2 * C]
    ext = jnp.concatenate([tail_ref[...], xpre], axis=0)
    xb = cb_ref[...] + cw_ref[CONV_WIDTH - 1:CONV_WIDTH, :] * xpre
    for k in range(1, CONV_WIDTH):
        w_k = cw_ref[CONV_WIDTH - 1 - k:CONV_WIDTH - k, :]
        xb = xb + w_k * pltpu.roll(ext, k, axis=0)[SUBLANES:, :]
    tail_ref[...] = xpre[SEQ_TILE - SUBLANES:, :]

    gates = jnp.dot(xb.astype(jnp.bfloat16), wg_ref[...],
                    preferred_element_type=jnp.float32) + gb_ref[...]
    r = jax.nn.sigmoid(gates[:, 0:C])
    ig = jax.nn.sigmoid(gates[:, C:2 * C])
    neg_lam = -lam_ref[...]
    softplus = jnp.maximum(neg_lam, 0.0) + jnp.log1p(jnp.exp(-jnp.abs(neg_lam)))
    log_a = -LRU_C * r * softplus
    a = jnp.exp(log_a)
    u = jnp.sqrt(1.0 - jnp.exp(2.0 * log_a)) * (ig * xb)
    d = 1
    while d < SEQ_TILE:
        u = u + a * _shift_rows(u, d, 0.0)
        a = a * _shift_rows(a, d, 1.0)
        d *= 2
    h = u + a * h_ref[0:1, :]
    h_ref[0:1, :] = h[SEQ_TILE - 1:SEQ_TILE, :]
    o_ref[...] = h * jax.nn.gelu(yb)


def rg_lru_mix(proj, conv_w, conv_b, gate_w, gate_b, lam, batch):
    T = proj.shape[0]
    C = LRU_WIDTH
    S = T // batch
    nt = S // SEQ_TILE
    eye = jnp.eye(LRU_BLOCKS, dtype=gate_w.dtype)
    wg = jnp.einsum('gnde,nm->ndgme', gate_w, eye).reshape(C, 2 * C).astype(jnp.bfloat16)
    full = lambda shape: pl.BlockSpec(shape, lambda b, j: (0, 0))
    return pl.pallas_call(
        _rg_lru_kernel,
        grid=(batch, nt),
        in_specs=[
            pl.BlockSpec((SEQ_TILE, 2 * C), lambda b, j: (b * nt + j, 0)),
            full((CONV_WIDTH, C)), full((1, C)), full((C, 2 * C)), full((1, 2 * C)), full((1, C)),
        ],
        out_specs=pl.BlockSpec((SEQ_TILE, C), lambda b, j: (b * nt + j, 0)),
        out_shape=jax.ShapeDtypeStruct((T, C), jnp.float32),
        scratch_shapes=[pltpu.VMEM((SUBLANES, C), jnp.float32), pltpu.VMEM((SUBLANES, C), jnp.float32)],
        compiler_params=pltpu.CompilerParams(
            dimension_semantics=("arbitrary", "arbitrary"), vmem_limit_bytes=VMEM_LIMIT_BYTES),
        name="rg_lru",
    )(proj, conv_w, conv_b.reshape(1, C), wg, gate_b.reshape(1, 2 * C), lam.reshape(1, C))


def _head_rmsnorm(x, gain):
    return x * lax.rsqrt(jnp.mean(x * x, axis=-1, keepdims=True) + RMS_EPS) * gain


def _mem_out_kernel(mix_ref, qm_ref, kv_ref, qg_ref, kg_ref, w_ref, x_ref, o_ref):
    bf16 = jnp.bfloat16
    acc = x_ref[...] + jnp.dot(mix_ref[...].astype(bf16), w_ref[0:MIX_WIDTH - MEM_WIDTH, :],
                               preferred_element_type=jnp.float32)
    for h in range(MEM_HEADS):
        cols = slice(h * HEAD_DIM, (h + 1) * HEAD_DIM)
        q = _head_rmsnorm(qm_ref[:, cols], qg_ref[...]) * (HEAD_DIM ** -0.5)
        k = _head_rmsnorm(kv_ref[:, cols], kg_ref[...])
        v = kv_ref[:, MEM_WIDTH + h * HEAD_DIM:MEM_WIDTH + (h + 1) * HEAD_DIM]
        s = lax.dot_general(q.astype(bf16), k.astype(bf16), (((1,), (1,)), ((), ())),
                            preferred_element_type=jnp.float32)
        e = jnp.exp(s - jnp.max(s, axis=-1, keepdims=True))
        p = e / jnp.sum(e, axis=-1, keepdims=True)
        o = jnp.dot(p.astype(bf16), v.astype(bf16), preferred_element_type=jnp.float32)
        row0 = MIX_WIDTH - MEM_WIDTH + h * HEAD_DIM
        acc = acc + jnp.dot(o.astype(bf16), w_ref[row0:row0 + HEAD_DIM, :],
                            preferred_element_type=jnp.float32)
    o_ref[...] = acc


def mem_out(mix, qm, kv, q_gain, k_gain, w_out, x2d, batch):
    T, D = x2d.shape
    S = T // batch
    nt = S // SEQ_TILE
    M = kv.shape[0] // batch
    row = lambda width: pl.BlockSpec((SEQ_TILE, width), lambda b, j: (b * nt + j, 0))
    full = lambda shape: pl.BlockSpec(shape, lambda b, j: (0, 0))
    return pl.pallas_call(
        _mem_out_kernel,
        grid=(batch, nt),
        in_specs=[
            row(MIX_WIDTH - MEM_WIDTH), row(MEM_WIDTH),
            pl.BlockSpec((M, 2 * MEM_WIDTH), lambda b, j: (b, 0)),
            full((1, HEAD_DIM)), full((1, HEAD_DIM)), full((MIX_WIDTH, D)), row(D),
        ],
        out_specs=row(D),
        out_shape=jax.ShapeDtypeStruct((T, D), jnp.float32),
        compiler_params=pltpu.CompilerParams(
            dimension_semantics=("arbitrary", "arbitrary"), vmem_limit_bytes=VMEM_LIMIT_BYTES),
        name="mem_out",
    )(mix, qm, kv, q_gain.reshape(1, HEAD_DIM), k_gain.reshape(1, HEAD_DIM),
      w_out.astype(jnp.bfloat16), x2d)


PEER_TOKENS_PER_STEP = 8
PEER_ROWS_PER_TOKEN = PEER_HEADS * PEER_TOPK
PEER_ROWS_PER_STEP = PEER_TOKENS_PER_STEP * PEER_ROWS_PER_TOKEN
PEER_GROUPS_PER_TOKEN = PEER_ROWS_PER_TOKEN // SUBLANES
PEER_GROUPS_PER_STEP = PEER_ROWS_PER_STEP // SUBLANES


LANES = 128
PEER_CHUNKS_PER_ROW = 2 * D_MODEL // LANES
PEER_U_CHUNKS = D_MODEL // LANES


PEER_SLOTS = 3


def _peer_gather_kernel(e_hbm, x_ref, gain_ref, gt_ref, uv_hbm, o_ref,
                        idx_smem, buf, idx_sem, row_sem):
    i = pl.program_id(0)
    n = pl.num_programs(0)
    slot = lax.rem(i, PEER_SLOTS)
    slot1 = lax.rem(i + 1, PEER_SLOTS)
    slot2 = lax.rem(i + 2, PEER_SLOTS)
    R = PEER_ROWS_PER_STEP
    TB = PEER_TOKENS_PER_STEP

    def idx_copy(step, s):
        return pltpu.make_async_copy(e_hbm.at[pl.ds(step * R, R)],
                                     idx_smem.at[pl.ds(s * R, R)], idx_sem.at[s])

    def row_copy(idx, s, grp, c):
        return pltpu.make_async_copy(uv_hbm.at[idx], buf.at[s, grp, :, c, :], row_sem.at[s])

    def slot_wait(s):
        pltpu.make_async_copy(buf.at[s], buf.at[s], row_sem.at[s]).wait()

    @pl.when(i == 0)
    def _():
        idx_copy(0, 0).start()
        idx_copy(1, 1).start()
        idx_copy(0, 0).wait()
        idx_copy(1, 1).wait()
        idx_copy(2, 2).start()

        def body(grp, carry):
            for c in range(SUBLANES):
                row_copy(idx_smem[grp * SUBLANES + c], grp // PEER_GROUPS_PER_STEP,
                         lax.rem(grp, PEER_GROUPS_PER_STEP), c).start(priority=c % 2)
            return carry
        lax.fori_loop(0, 2 * PEER_GROUPS_PER_STEP, body, 0)

    idx_copy(i + 2, slot2).wait()

    @pl.when(i + 3 <= n + 1)
    def _():
        idx_copy(i + 3, slot).start()

    def issue_token_rows(t):
        base = slot2 * R + t * PEER_ROWS_PER_TOKEN
        for gi in range(PEER_GROUPS_PER_TOKEN):
            for c in range(SUBLANES):
                row_copy(idx_smem[base + gi * SUBLANES + c], slot2,
                         t * PEER_GROUPS_PER_TOKEN + gi, c).start(priority=c % 2)

    issue_token_rows(0)
    slot_wait(slot)

    x = x_ref[...]
    ms = jnp.sum(jnp.sum(x * x, axis=2, keepdims=True), axis=1, keepdims=True) * (1.0 / D_MODEL)
    xn = x * lax.rsqrt(ms + RMS_EPS) * gain_ref[...]
    outs = []
    for t in range(TB):
        blk = buf[slot, pl.ds(t * PEER_GROUPS_PER_TOKEN, PEER_GROUPS_PER_TOKEN)]
        u = blk[:, :PEER_U_CHUNKS]
        v = blk[:, PEER_U_CHUNKS:]
        h = jnp.sum(jnp.sum(u * xn[t][None, :, None, :], axis=1), axis=-1, keepdims=True)
        a = jax.nn.gelu(h) * gt_ref[:, t:t + 1].reshape(PEER_GROUPS_PER_TOKEN, SUBLANES, 1)
        o = jnp.sum(jnp.sum(a[:, None, :, :] * v, axis=0), axis=1)
        outs.append(x[t] + o)
        if t + 1 < TB:
            issue_token_rows(t + 1)
    o_ref[...] = jnp.stack(outs, axis=0)

    @pl.when(i == n - 1)
    def _():
        slot_wait(slot1)
        slot_wait(slot2)


def peer_gather(x2d, gain, e, g, uv):
    T, D = x2d.shape
    TB = PEER_TOKENS_PER_STEP
    R = PEER_ROWS_PER_STEP
    n_steps = T // TB
    gt = g.reshape(n_steps, TB, PEER_ROWS_PER_TOKEN).transpose(0, 2, 1)
    uv3 = uv.reshape(uv.shape[0], PEER_CHUNKS_PER_ROW, LANES)
    e_flat = jnp.concatenate([e.reshape(T * PEER_ROWS_PER_TOKEN),
                              jnp.zeros(((PEER_SLOTS - 1) * R,), jnp.int32)])
    out = pl.pallas_call(
        _peer_gather_kernel,
        grid=(n_steps,),
        in_specs=[
            pl.BlockSpec(memory_space=pl.ANY),
            pl.BlockSpec((TB, PEER_U_CHUNKS, LANES), lambda i: (i, 0, 0)),
            pl.BlockSpec((PEER_U_CHUNKS, LANES), lambda i: (0, 0)),
            pl.BlockSpec((None, PEER_ROWS_PER_TOKEN, TB), lambda i: (i, 0, 0)),
            pl.BlockSpec(memory_space=pl.ANY),
        ],
        out_specs=pl.BlockSpec((TB, PEER_U_CHUNKS, LANES), lambda i: (i, 0, 0)),
        out_shape=jax.ShapeDtypeStruct((T, PEER_U_CHUNKS, LANES), jnp.float32),
        scratch_shapes=[
            pltpu.SMEM((PEER_SLOTS * R,), jnp.int32),
            pltpu.VMEM((PEER_SLOTS, PEER_GROUPS_PER_STEP, PEER_CHUNKS_PER_ROW, SUBLANES, LANES), jnp.float32),
            pltpu.SemaphoreType.DMA((PEER_SLOTS,)),
            pltpu.SemaphoreType.DMA((PEER_SLOTS,)),
        ],
        compiler_params=pltpu.CompilerParams(
            dimension_semantics=("arbitrary",), vmem_limit_bytes=VMEM_LIMIT_BYTES),
        name="peer_gather",
    )(e_flat, x2d.reshape(T, PEER_U_CHUNKS, LANES), gain.reshape(PEER_U_CHUNKS, LANES), gt, uv3)
    return out.reshape(T, D)


PEER_ROUTE_TOKENS = 256
PEER_HALF = PEER_KEY_DIM // 2


def _top16_rows(s, ids=None):
    if ids is None:
        ids = lax.broadcasted_iota(jnp.int32, s.shape, 0)
    vals, idxs = [], []
    for _ in range(PEER_TOPK):
        m = jnp.max(s, axis=0, keepdims=True)
        idx = jnp.min(jnp.where(s == m, ids, jnp.iinfo(jnp.int32).max), axis=0, keepdims=True)
        vals.append(m)
        idxs.append(idx)
        s = jnp.where(ids == idx, -jnp.inf, s)
    return jnp.concatenate(vals, axis=0), jnp.concatenate(idxs, axis=0)


def _pair_candidates(v1, v2):
    n = PEER_TOPK
    tokens = v1.shape[1]
    sums, ids = [], []

    def add(piece, a_of_row, b_of_row):
        ok = (a_of_row + 1) * (b_of_row + 1) <= n
        sums.append(jnp.where(ok, piece, -jnp.inf))
        ids.append(a_of_row * n + b_of_row)

    for a in range(4):
        rows = n if a == 0 else SUBLANES
        b_row = lax.broadcasted_iota(jnp.int32, (rows, tokens), 0)
        add(v1[a:a + 1, :] + v2[0:rows, :], jnp.full_like(b_row, a), b_row)
    for b in range(3):
        rows = n if b == 0 else SUBLANES
        a_row = lax.broadcasted_iota(jnp.int32, (rows, tokens), 0)
        piece = jnp.where(a_row >= 4, v2[b:b + 1, :] + v1[0:rows, :], -jnp.inf)
        add(piece, a_row, jnp.full_like(a_row, b))
    return jnp.concatenate(sums, axis=0), jnp.concatenate(ids, axis=0)


def _pick_rows(table, which):
    out = jnp.zeros_like(table)
    for a in range(PEER_TOPK):
        out = out + jnp.where(which == a, table[a:a + 1, :], 0)
    return out


def _peer_route_kernel(q_ref, keys_ref, e_ref, g_ref):
    qb = q_ref[...].astype(jnp.bfloat16)
    s1 = lax.dot_general(keys_ref[0], qb[:, :PEER_HALF], (((1,), (1,)), ((), ())),
                         preferred_element_type=jnp.float32)
    s2 = lax.dot_general(keys_ref[1], qb[:, PEER_HALF:], (((1,), (1,)), ((), ())),
                         preferred_element_type=jnp.float32)
    v1, i1 = _top16_rows(s1)
    v2, i2 = _top16_rows(s2)
    vals, ci = _top16_rows(*_pair_candidates(v1, v2))
    e_ref[...] = (_pick_rows(i1, lax.shift_right_logical(ci, 4)) * N_KEYS
                  + _pick_rows(i2, ci & (PEER_TOPK - 1)))
    p = jnp.exp(vals - vals[0:1, :])
    g_ref[...] = p / jnp.sum(p, axis=0, keepdims=True)


def peer_route(q_all, sub_keys):
    T = q_all.shape[0]
    TT = PEER_ROUTE_TOKENS
    return pl.pallas_call(
        _peer_route_kernel,
        grid=(T // TT, PEER_HEADS),
        in_specs=[
            pl.BlockSpec((TT, PEER_KEY_DIM), lambda i, h: (i, h)),
            pl.BlockSpec((None, 2, N_KEYS, PEER_HALF), lambda i, h: (h, 0, 0, 0)),
        ],
        out_specs=[
            pl.BlockSpec((PEER_TOPK, TT), lambda i, h: (h, i)),
            pl.BlockSpec((PEER_TOPK, TT), lambda i, h: (h, i)),
        ],
        out_shape=[jax.ShapeDtypeStruct((PEER_ROWS_PER_TOKEN, T), jnp.int32),
                   jax.ShapeDtypeStruct((PEER_ROWS_PER_TOKEN, T), jnp.float32)],
        compiler_params=pltpu.CompilerParams(
            dimension_semantics=("arbitrary", "arbitrary"), vmem_limit_bytes=VMEM_LIMIT_BYTES),
        name="peer_route",
    )(q_all, sub_keys.astype(jnp.bfloat16))


NSA_GROUP = NSA_HEADS // NSA_KV_HEADS
NSA_ROWS = NSA_GROUP * Q_CHUNK
NSA_KEY_TILE = 512


def _qk(qb, k):
    return lax.dot_general(qb, k, (((1,), (1,)), ((), ())), preferred_element_type=jnp.float32)


def _nsa_kernel(q_ref, gl_ref, kc_ref, vc_ref, ov_ref, ks_ref, vs_ref, kw_ref, vw_ref,
                o_ref, mexp_ref):
    c = pl.program_id(1)
    rows = NSA_ROWS
    n_sel = mexp_ref.shape[0] * (NSA_KEY_TILE // SEL_LEN)
    n_cmp_pad = kc_ref.shape[0]
    f32 = jnp.float32
    bf16 = jnp.bfloat16

    qb = (q_ref[...].reshape(rows, HEAD_DIM) * (HEAD_DIM ** -0.5)).astype(bf16)
    t = c * Q_CHUNK + lax.broadcasted_iota(jnp.int32, (Q_CHUNK, 1), 0)

    def bias_rows(allowed):
        return jnp.concatenate([jnp.where(allowed, 0.0, NEG_INF)] * NSA_GROUP, axis=0)

    cmp_end = lax.broadcasted_iota(jnp.int32, (1, n_cmp_pad), 1) * CMP_STRIDE + (CMP_LEN - 1)
    s = _qk(qb, kc_ref[...]) + bias_rows(cmp_end <= t)
    m = jnp.maximum(jnp.max(s, axis=-1, keepdims=True), MASK_FLOOR)
    e = jnp.exp(s - m)
    l = jnp.sum(e, axis=-1, keepdims=True)
    pb = (e / jnp.where(l > 0.0, l, 1.0)).astype(bf16)
    o_cmp = jnp.dot(pb, vc_ref[...], preferred_element_type=f32)

    imp = jnp.dot(pb, ov_ref[...], preferred_element_type=f32)
    imp = jnp.sum(imp.reshape(NSA_GROUP, Q_CHUNK, n_sel), axis=0)
    j = lax.broadcasted_iota(jnp.int32, (Q_CHUNK, n_sel), 1)
    forced = (j == 0) | (j == c) | (j == c - 1)
    imp = jnp.where(forced, SEL_FORCE_SCORE, imp)
    imp = jnp.where(j <= c, imp, -1.0)
    rank = jnp.zeros((Q_CHUNK, n_sel), f32)
    for jp in range(n_sel):
        col = imp[:, jp:jp + 1]
        beats = (col > imp) | ((col == imp) & (j > jp))
        rank = rank + jnp.where(beats, 1.0, 0.0)
    sel = jnp.where((rank < float(SEL_TOP)) & (imp >= 0.0), 1.0, 0.0).astype(bf16)
    blk_of_key = lax.broadcasted_iota(jnp.int32, (n_sel, NSA_KEY_TILE), 1) // SEL_LEN
    blk_row = lax.broadcasted_iota(jnp.int32, (n_sel, NSA_KEY_TILE), 0)
    key_in_tile = lax.broadcasted_iota(jnp.int32, (1, NSA_KEY_TILE), 1)
    for kt in range(mexp_ref.shape[0]):
        expand = jnp.where(blk_of_key + kt * (NSA_KEY_TILE // SEL_LEN) == blk_row, 1.0, 0.0).astype(bf16)
        picked = jnp.dot(sel, expand, preferred_element_type=f32) > 0.5
        mexp_ref[kt] = jnp.where(picked & (key_in_tile + kt * NSA_KEY_TILE <= t), 0.0, NEG_INF)

    def sel_tile(kt, carry):
        m_i, l_i, acc = carry
        off = pl.multiple_of(kt * NSA_KEY_TILE, NSA_KEY_TILE)
        s = (_qk(qb, ks_ref[pl.ds(off, NSA_KEY_TILE), :])
             + jnp.concatenate([mexp_ref[kt]] * NSA_GROUP, axis=0))
        m_new = jnp.maximum(m_i, jnp.max(s, axis=-1, keepdims=True))
        alpha = jnp.exp(m_i - m_new)
        p = jnp.exp(s - m_new)
        l_new = alpha * l_i + jnp.sum(p, axis=-1, keepdims=True)
        acc_new = alpha * acc + jnp.dot(p.astype(bf16), vs_ref[pl.ds(off, NSA_KEY_TILE), :],
                                        preferred_element_type=f32)
        return m_new, l_new, acc_new

    n_tiles = c // (NSA_KEY_TILE // SEL_LEN) + 1
    init = (jnp.full((rows, 1), MASK_FLOOR, f32), jnp.zeros((rows, 1), f32),
            jnp.zeros((rows, HEAD_DIM), f32))
    _, l_s, acc_s = lax.fori_loop(0, n_tiles, sel_tile, init)
    o_sel = acc_s / l_s

    start = pl.multiple_of(jnp.maximum(c * Q_CHUNK - WINDOW, 0), Q_CHUNK)
    cur = pl.multiple_of(c * Q_CHUNK, Q_CHUNK)
    pos_a = start + lax.broadcasted_iota(jnp.int32, (1, WINDOW), 1)
    pos_b = cur + lax.broadcasted_iota(jnp.int32, (1, Q_CHUNK), 1)
    s_a = (_qk(qb, kw_ref[pl.ds(start, WINDOW), :])
           + bias_rows((pos_a < cur) & (pos_a > t - WINDOW)))
    s_b = _qk(qb, kw_ref[pl.ds(cur, Q_CHUNK), :]) + bias_rows(pos_b <= t)
    m_w = jnp.maximum(jnp.max(s_a, axis=-1, keepdims=True), jnp.max(s_b, axis=-1, keepdims=True))
    p_a = jnp.exp(s_a - m_w)
    p_b = jnp.exp(s_b - m_w)
    l_w = jnp.sum(p_a, axis=-1, keepdims=True) + jnp.sum(p_b, axis=-1, keepdims=True)
    o_win = (jnp.dot(p_a.astype(bf16), vw_ref[pl.ds(start, WINDOW), :], preferred_element_type=f32)
             + jnp.dot(p_b.astype(bf16), vw_ref[pl.ds(cur, Q_CHUNK), :], preferred_element_type=f32)) / l_w

    g = jax.nn.sigmoid(gl_ref[...])
    o = g[:, 0:1] * o_cmp + g[:, 1:2] * o_sel + g[:, 2:3] * o_win
    o_ref[...] = o.reshape(NSA_GROUP, Q_CHUNK, HEAD_DIM)


def nsa_attention(q, gl, kc, vc, ks, vs, kw, vw):
    B, S, H, dh = q.shape
    G = NSA_KV_HEADS
    R = NSA_GROUP
    NQ = S // Q_CHUNK
    n_sel = S // SEL_LEN
    n_cmp = kc.shape[2]
    n_cmp_pad = -(-n_cmp // 128) * 128
    bf16 = jnp.bfloat16
    qt = q.reshape(B, S, G, R, dh).transpose(0, 2, 3, 1, 4).reshape(B * G, R, S, dh)
    glt = gl.reshape(B, NQ, Q_CHUNK, G, R, N_BRANCH).transpose(0, 3, 1, 4, 2, 5)
    glt = glt.reshape(B * G, NQ, NSA_ROWS, N_BRANCH)
    pad_c = lambda a: jnp.pad(a.reshape(B * G, n_cmp, dh), ((0, 0), (0, n_cmp_pad - n_cmp), (0, 0))).astype(bf16)
    flat = lambda a: a.reshape(B * G, S, dh).astype(bf16)
    c0 = np.arange(n_cmp_pad)[:, None] * CMP_STRIDE
    s0 = np.arange(n_sel)[None, :] * SEL_LEN
    ov = np.clip(np.minimum(c0 + CMP_LEN, s0 + SEL_LEN) - np.maximum(c0, s0), 0, None) / CMP_LEN
    ov[n_cmp:] = 0.0
    kv_spec = pl.BlockSpec((None, S, dh), lambda bg, c: (bg, 0, 0))
    cmp_spec = pl.BlockSpec((None, n_cmp_pad, dh), lambda bg, c: (bg, 0, 0))
    out = pl.pallas_call(
        _nsa_kernel,
        grid=(B * G, NQ),
        in_specs=[
            pl.BlockSpec((None, R, Q_CHUNK, dh), lambda bg, c: (bg, 0, c, 0)),
            pl.BlockSpec((None, None, NSA_ROWS, N_BRANCH), lambda bg, c: (bg, c, 0, 0)),
            cmp_spec, cmp_spec,
            pl.BlockSpec((n_cmp_pad, n_sel), lambda bg, c: (0, 0)),
            kv_spec, kv_spec, kv_spec, kv_spec,
        ],
        out_specs=pl.BlockSpec((None, R, Q_CHUNK, dh), lambda bg, c: (bg, 0, c, 0)),
        out_shape=jax.ShapeDtypeStruct((B * G, R, S, dh), jnp.float32),
        scratch_shapes=[pltpu.VMEM((S // NSA_KEY_TILE, Q_CHUNK, NSA_KEY_TILE), jnp.float32)],
        compiler_params=pltpu.CompilerParams(
            dimension_semantics=("arbitrary", "arbitrary"), vmem_limit_bytes=VMEM_LIMIT_BYTES),
        name="nsa_attention",
    )(qt, glt, pad_c(kc), pad_c(vc), jnp.asarray(ov, bf16), flat(ks), flat(vs), flat(kw), flat(vw))
    return out.reshape(B, G, R, S, dh).transpose(0, 3, 1, 2, 4).reshape(B, S, H * dh)


def _rmsnorm(x, g):
    y = x * lax.rsqrt(jnp.mean(x * x, axis=-1, keepdims=True) + RMS_EPS)
    return y * g


def _rope(x, pos):
    half = ROPE_DIM // 2
    freqs = ROPE_THETA ** (-jnp.arange(half, dtype=jnp.float32) / half)
    ang = pos.astype(jnp.float32)[:, None] * freqs[None, :]
    cos = jnp.cos(ang)[:, None, :]
    sin = jnp.sin(ang)[:, None, :]
    x1, x2, rest = x[..., :half], x[..., half:ROPE_DIM], x[..., ROPE_DIM:]
    return jnp.concatenate([x1 * cos - x2 * sin, x2 * cos + x1 * sin, rest], axis=-1)


def _nsa_shared_kv(x, positions, kv_norm, w_kv_shared, k_gain_shared, cmp_pos, cmp_w1, cmp_b1, cmp_w2):
    B, S, _ = x.shape
    G, dh = NSA_KV_HEADS, HEAD_DIM
    kv = norm_matmul(x.reshape(B * S, D_MODEL), kv_norm, w_kv_shared).reshape(B, S, N_BRANCH, 2, G, dh)
    n_cmp = (S - CMP_LEN) // CMP_STRIDE + 1
    idx = np.arange(n_cmp)[:, None] * CMP_STRIDE + np.arange(CMP_LEN)[None, :]

    def compress(tok, j):
        blk = tok[:, idx] + cmp_pos[j][None, None, :, None, :]
        blk = blk.transpose(0, 1, 3, 2, 4).reshape(B, n_cmp, G, CMP_LEN * dh)
        hid = jax.nn.gelu(blk @ cmp_w1[j] + cmp_b1[j])
        return hid @ cmp_w2[j]

    kc = _rope(_rmsnorm(compress(kv[:, :, 0, 0], 0), k_gain_shared[0]), positions[idx[:, -1]])
    vc = compress(kv[:, :, 0, 1], 1)
    ks = _rope(_rmsnorm(kv[:, :, 1, 0], k_gain_shared[1]), positions)
    vs = kv[:, :, 1, 1]
    kw = _rope(_rmsnorm(kv[:, :, 2, 0], k_gain_shared[2]), positions)
    vw = kv[:, :, 2, 1]
    t = lambda a: a.transpose(0, 2, 1, 3)
    return (t(kc), t(vc), t(ks), t(vs), t(kw), t(vw))


def _peer_residual(x2d, norm_g, w_q, sub_keys, u_tab, v_tab):
    q_all = norm_matmul(x2d, norm_g, w_q)
    e_t, g_t = peer_route(q_all, sub_keys)
    uv = jnp.concatenate([u_tab, v_tab], axis=1)
    return peer_gather(x2d, norm_g, e_t.T, g_t.T, uv)


def kernel(x, mem, positions, norm_mix, norm_ffn, norm_mem, w_out, w_mem_kv, mem_q_gain, mem_k_gain, a_w_in, a_conv_w, a_conv_b, a_gate_w, a_gate_b, a_lambda, b_w_in, b_gate_b, b_q_gain, kv_norm, w_kv_shared, k_gain_shared, cmp_pos, cmp_w1, cmp_b1, cmp_w2, peer_wq, peer_subkeys, peer_u, peer_v):
    B, S, D = x.shape
    T = B * S
    x2d = x.reshape(T, D)
    shared = None
    for l in range(DEPTH):
        if l < N_A:
            proj = norm_matmul(x2d, norm_mix[l], a_w_in[l])
            qm = proj[:, 2 * LRU_WIDTH:]
            mix = rg_lru_mix(proj, a_conv_w[l], a_conv_b[l], a_gate_w[l], a_gate_b[l], a_lambda[l], B)
        else:
            j = l - N_A
            proj = norm_matmul(x2d, norm_mix[l], b_w_in[j])
            qm = proj[:, NSA_WIDTH + N_BRANCH * NSA_HEADS:]
            proj = proj.reshape(B, S, -1)
            q = proj[..., :NSA_WIDTH].reshape(B, S, NSA_HEADS, HEAD_DIM)
            gl = (proj[..., NSA_WIDTH:NSA_WIDTH + N_BRANCH * NSA_HEADS].reshape(B, S, NSA_HEADS, N_BRANCH)
                  + b_gate_b[j].reshape(NSA_HEADS, N_BRANCH))
            q = _rope(_rmsnorm(q, b_q_gain[j]), positions)
            mix = nsa_attention(q, gl, *shared).reshape(T, NSA_WIDTH)
        kv_mem = norm_matmul(mem.reshape(-1, D), norm_mem[l], w_mem_kv[l])
        x2d = mem_out(mix, qm, kv_mem, mem_q_gain[l], mem_k_gain[l], w_out[l], x2d, B)
        x2d = _peer_residual(x2d, norm_ffn[l], peer_wq[l], peer_subkeys[l], peer_u[l], peer_v[l])
        if l == N_A - 1:
            shared = _nsa_shared_kv(x2d.reshape(B, S, D), positions, kv_norm, w_kv_shared, k_gain_shared,
                                    cmp_pos, cmp_w1, cmp_b1, cmp_w2)
    return x2d.reshape(B, S, D)
```

```python
import jax
import jax.numpy as jnp
import numpy as np
from jax import lax
from jax.experimental import pallas as pl
from jax.experimental.pallas import tpu as pltpu

D_MODEL = 1024
DEPTH = 2
N_A = DEPTH // 2
HEAD_DIM = 64
ROPE_DIM = HEAD_DIM // 4
ROPE_THETA = 500000.0
RMS_EPS = 1e-6
NEG_INF = -1e30
MASK_FLOOR = -1e29
MEM_HEADS = 4
MEM_WIDTH = MEM_HEADS * HEAD_DIM
MIX_WIDTH = D_MODEL
LRU_WIDTH = MIX_WIDTH - MEM_WIDTH
LRU_BLOCKS = LRU_WIDTH // HEAD_DIM
CONV_WIDTH = 4
LRU_C = 8.0
NSA_WIDTH = MIX_WIDTH - MEM_WIDTH
NSA_HEADS = NSA_WIDTH // HEAD_DIM
NSA_KV_HEADS = 2
N_BRANCH = 3
CMP_LEN = 32
CMP_STRIDE = 16
SEL_LEN = 64
SEL_TOP = 16
SEL_FORCE_SCORE = 1e4
WINDOW = 512
Q_CHUNK = 64
N_KEYS = 128
PEER_HEADS = 8
PEER_KEY_DIM = 256
PEER_TOPK = 16
PEER_CHUNK_MAX = 512

VMEM_LIMIT_BYTES = 48 * 1024 * 1024
ROW_TILE = 512


def _norm_matmul_kernel(x_ref, g_ref, w_ref, o_ref):
    x = x_ref[...]
    ms = jnp.mean(x * x, axis=-1, keepdims=True)
    y = x * lax.rsqrt(ms + RMS_EPS) * g_ref[...]
    o_ref[...] = jnp.dot(y.astype(jnp.bfloat16), w_ref[...],
                         preferred_element_type=jnp.float32)


def norm_matmul(x2d, g, w):
    T, D = x2d.shape
    N = w.shape[1]
    return pl.pallas_call(
        _norm_matmul_kernel,
        grid=(T // ROW_TILE,),
        in_specs=[
            pl.BlockSpec((ROW_TILE, D), lambda i: (i, 0)),
            pl.BlockSpec((1, D), lambda i: (0, 0)),
            pl.BlockSpec((D, N), lambda i: (0, 0)),
        ],
        out_specs=pl.BlockSpec((ROW_TILE, N), lambda i: (i, 0)),
        out_shape=jax.ShapeDtypeStruct((T, N), jnp.float32),
        compiler_params=pltpu.CompilerParams(
            dimension_semantics=("arbitrary",), vmem_limit_bytes=VMEM_LIMIT_BYTES),
        name="norm_matmul",
    )(x2d, g.reshape(1, D), w.astype(jnp.bfloat16))


SEQ_TILE = 256
SUBLANES = 8


def _shift_rows(x, k, fill):
    rolled = pltpu.roll(x, k, axis=0)
    row = lax.broadcasted_iota(jnp.int32, x.shape, 0)
    return jnp.where(row >= k, rolled, fill)


def _rg_lru_kernel(p_ref, cw_ref, cb_ref, wg_ref, gb_ref, lam_ref, o_ref, tail_ref, h_ref):
    C = LRU_WIDTH

    @pl.when(pl.program_id(1) == 0)
    def _():
        tail_ref[...] = jnp.zeros_like(tail_ref)
        h_ref[...] = jnp.zeros_like(h_ref)

    xpre = p_ref[:, 0:C]
    yb = p_ref[:, C:2 * C]
    ext = jnp.concatenate([tail_ref[...], xpre], axis=0)
    xb = cb_ref[...] + cw_ref[CONV_WIDTH - 1:CONV_WIDTH, :] * xpre
    for k in range(1, CONV_WIDTH):
        w_k = cw_ref[CONV_WIDTH - 1 - k:CONV_WIDTH - k, :]
        xb = xb + w_k * pltpu.roll(ext, k, axis=0)[SUBLANES:, :]
    tail_ref[...] = xpre[SEQ_TILE - SUBLANES:, :]

    gates = jnp.dot(xb.astype(jnp.bfloat16), wg_ref[...],
                    preferred_element_type=jnp.float32) + gb_ref[...]
    r = jax.nn.sigmoid(gates[:, 0:C])
    ig = jax.nn.sigmoid(gates[:, C:2 * C])
    neg_lam = -lam_ref[...]
    softplus = jnp.maximum(neg_lam, 0.0) + jnp.log1p(jnp.exp(-jnp.abs(neg_lam)))
    log_a = -LRU_C * r * softplus
    a = jnp.exp(log_a)
    u = jnp.sqrt(1.0 - jnp.exp(2.0 * log_a)) * (ig * xb)
    d = 1
    while d < SEQ_TILE:
        u = u + a * _shift_rows(u, d, 0.0)
        a = a * _shift_rows(a, d, 1.0)
        d *= 2
    h = u + a * h_ref[0:1, :]
    h_ref[0:1, :] = h[SEQ_TILE - 1:SEQ_TILE, :]
    o_ref[...] = h * jax.nn.gelu(yb)


def rg_lru_mix(proj, conv_w, conv_b, gate_w, gate_b, lam, batch):
    T = proj.shape[0]
    C = LRU_WIDTH
    S = T // batch
    nt = S // SEQ_TILE
    eye = jnp.eye(LRU_BLOCKS, dtype=gate_w.dtype)
    wg = jnp.einsum('gnde,nm->ndgme', gate_w, eye).reshape(C, 2 * C).astype(jnp.bfloat16)
    full = lambda shape: pl.BlockSpec(shape, lambda b, j: (0, 0))
    return pl.pallas_call(
        _rg_lru_kernel,
        grid=(batch, nt),
        in_specs=[
            pl.BlockSpec((SEQ_TILE, 2 * C), lambda b, j: (b * nt + j, 0)),
            full((CONV_WIDTH, C)), full((1, C)), full((C, 2 * C)), full((1, 2 * C)), full((1, C)),
        ],
        out_specs=pl.BlockSpec((SEQ_TILE, C), lambda b, j: (b * nt + j, 0)),
        out_shape=jax.ShapeDtypeStruct((T, C), jnp.float32),
        scratch_shapes=[pltpu.VMEM((SUBLANES, C), jnp.float32), pltpu.VMEM((SUBLANES, C), jnp.float32)],
        compiler_params=pltpu.CompilerParams(
            dimension_semantics=("arbitrary", "arbitrary"), vmem_limit_bytes=VMEM_LIMIT_BYTES),
        name="rg_lru",
    )(proj, conv_w, conv_b.reshape(1, C), wg, gate_b.reshape(1, 2 * C), lam.reshape(1, C))


def _head_rmsnorm(x, gain):
    return x * lax.rsqrt(jnp.mean(x * x, axis=-1, keepdims=True) + RMS_EPS) * gain


def _mem_out_kernel(mix_ref, qm_ref, kv_ref, qg_ref, kg_ref, w_ref, x_ref, o_ref):
    bf16 = jnp.bfloat16
    acc = x_ref[...] + jnp.dot(mix_ref[...].astype(bf16), w_ref[0:MIX_WIDTH - MEM_WIDTH, :],
                               preferred_element_type=jnp.float32)
    for h in range(MEM_HEADS):
        cols = slice(h * HEAD_DIM, (h + 1) * HEAD_DIM)
        q = _head_rmsnorm(qm_ref[:, cols], qg_ref[...]) * (HEAD_DIM ** -0.5)
        k = _head_rmsnorm(kv_ref[:, cols], kg_ref[...])
        v = kv_ref[:, MEM_WIDTH + h * HEAD_DIM:MEM_WIDTH + (h + 1) * HEAD_DIM]
        s = lax.dot_general(q.astype(bf16), k.astype(bf16), (((1,), (1,)), ((), ())),
                            preferred_element_type=jnp.float32)
        e = jnp.exp(s - jnp.max(s, axis=-1, keepdims=True))
        p = e / jnp.sum(e, axis=-1, keepdims=True)
        o = jnp.dot(p.astype(bf16), v.astype(bf16), preferred_element_type=jnp.float32)
        row0 = MIX_WIDTH - MEM_WIDTH + h * HEAD_DIM
        acc = acc + jnp.dot(o.astype(bf16), w_ref[row0:row0 + HEAD_DIM, :],
                            preferred_element_type=jnp.float32)
    o_ref[...] = acc


def mem_out(mix, qm, kv, q_gain, k_gain, w_out, x2d, batch):
    T, D = x2d.shape
    S = T // batch
    nt = S // SEQ_TILE
    M = kv.shape[0] // batch
    row = lambda width: pl.BlockSpec((SEQ_TILE, width), lambda b, j: (b * nt + j, 0))
    full = lambda shape: pl.BlockSpec(shape, lambda b, j: (0, 0))
    return pl.pallas_call(
        _mem_out_kernel,
        grid=(batch, nt),
        in_specs=[
            row(MIX_WIDTH - MEM_WIDTH), row(MEM_WIDTH),
            pl.BlockSpec((M, 2 * MEM_WIDTH), lambda b, j: (b, 0)),
            full((1, HEAD_DIM)), full((1, HEAD_DIM)), full((MIX_WIDTH, D)), row(D),
        ],
        out_specs=row(D),
        out_shape=jax.ShapeDtypeStruct((T, D), jnp.float32),
        compiler_params=pltpu.CompilerParams(
            dimension_semantics=("arbitrary", "arbitrary"), vmem_limit_bytes=VMEM_LIMIT_BYTES),
        name="mem_out",
    )(mix, qm, kv, q_gain.reshape(1, HEAD_DIM), k_gain.reshape(1, HEAD_DIM),
      w_out.astype(jnp.bfloat16), x2d)


PEER_TOKENS_PER_STEP = 8
PEER_ROWS_PER_TOKEN = PEER_HEADS * PEER_TOPK
PEER_ROWS_PER_STEP = PEER_TOKENS_PER_STEP * PEER_ROWS_PER_TOKEN
PEER_GROUPS_PER_TOKEN = PEER_ROWS_PER_TOKEN // SUBLANES
PEER_GROUPS_PER_STEP = PEER_ROWS_PER_STEP // SUBLANES


LANES = 128
PEER_CHUNKS_PER_ROW = 2 * D_MODEL // LANES
PEER_U_CHUNKS = D_MODEL // LANES


PEER_SLOTS = 3


def _peer_gather_kernel(e_hbm, x_ref, gain_ref, gt_ref, uv_hbm, o_ref,
                        idx_smem, buf, idx_sem, row_sem):
    i = pl.program_id(0)
    n = pl.num_programs(0)
    slot = lax.rem(i, PEER_SLOTS)
    slot1 = lax.rem(i + 1, PEER_SLOTS)
    slot2 = lax.rem(i + 2, PEER_SLOTS)
    R = PEER_ROWS_PER_STEP
    TB = PEER_TOKENS_PER_STEP

    def idx_copy(step, s):
        return pltpu.make_async_copy(e_hbm.at[pl.ds(step * R, R)],
                                     idx_smem.at[pl.ds(s * R, R)], idx_sem.at[s])

    def row_copy(idx, s, grp, c):
        return pltpu.make_async_copy(uv_hbm.at[idx], buf.at[s, grp, :, c, :], row_sem.at[s])

    def slot_wait(s):
        pltpu.make_async_copy(buf.at[s], buf.at[s], row_sem.at[s]).wait()

    @pl.when(i == 0)
    def _():
        idx_copy(0, 0).start()
        idx_copy(1, 1).start()
        idx_copy(0, 0).wait()
        idx_copy(1, 1).wait()
        idx_copy(2, 2).start()

        def body(grp, carry):
            for c in range(SUBLANES):
                row_copy(idx_smem[grp * SUBLANES + c], grp // PEER_GROUPS_PER_STEP,
                         lax.rem(grp, PEER_GROUPS_PER_STEP), c).start(priority=c % 2)
            return carry
        lax.fori_loop(0, 2 * PEER_GROUPS_PER_STEP, body, 0)

    idx_copy(i + 2, slot2).wait()

    @pl.when(i + 3 <= n + 1)
    def _():
        idx_copy(i + 3, slot).start()

    def issue_token_rows(t):
        base = slot2 * R + t * PEER_ROWS_PER_TOKEN
        for gi in range(PEER_GROUPS_PER_TOKEN):
            for c in range(SUBLANES):
                row_copy(idx_smem[base + gi * SUBLANES + c], slot2,
                         t * PEER_GROUPS_PER_TOKEN + gi, c).start(priority=c % 2)

    issue_token_rows(0)
    slot_wait(slot)

    x = x_ref[...]
    ms = jnp.sum(jnp.sum(x * x, axis=2, keepdims=True), axis=1, keepdims=True) * (1.0 / D_MODEL)
    xn = x * lax.rsqrt(ms + RMS_EPS) * gain_ref[...]
    outs = []
    for t in range(TB):
        blk = buf[slot, pl.ds(t * PEER_GROUPS_PER_TOKEN, PEER_GROUPS_PER_TOKEN)]
        u = blk[:, :PEER_U_CHUNKS]
        v = blk[:, PEER_U_CHUNKS:]
        h = jnp.sum(jnp.sum(u * xn[t][None, :, None, :], axis=1), axis=-1, keepdims=True)
        a = jax.nn.gelu(h) * gt_ref[:, t:t + 1].reshape(PEER_GROUPS_PER_TOKEN, SUBLANES, 1)
        o = jnp.sum(jnp.sum(a[:, None, :, :] * v, axis=0), axis=1)
        outs.append(x[t] + o)
        if t + 1 < TB:
            issue_token_rows(t + 1)
    o_ref[...] = jnp.stack(outs, axis=0)

    @pl.when(i == n - 1)
    def _():
        slot_wait(slot1)
        slot_wait(slot2)


def peer_gather(x2d, gain, e, g, uv):
    T, D = x2d.shape
    TB = PEER_TOKENS_PER_STEP
    R = PEER_ROWS_PER_STEP
    n_steps = T // TB
    gt = g.reshape(n_steps, TB, PEER_ROWS_PER_TOKEN).transpose(0, 2, 1)
    uv3 = uv.reshape(uv.shape[0], PEER_CHUNKS_PER_ROW, LANES)
    e_flat = jnp.concatenate([e.reshape(T * PEER_ROWS_PER_TOKEN),
                              jnp.zeros(((PEER_SLOTS - 1) * R,), jnp.int32)])
    out = pl.pallas_call(
        _peer_gather_kernel,
        grid=(n_steps,),
        in_specs=[
            pl.BlockSpec(memory_space=pl.ANY),
            pl.BlockSpec((TB, PEER_U_CHUNKS, LANES), lambda i: (i, 0, 0)),
            pl.BlockSpec((PEER_U_CHUNKS, LANES), lambda i: (0, 0)),
            pl.BlockSpec((None, PEER_ROWS_PER_TOKEN, TB), lambda i: (i, 0, 0)),
            pl.BlockSpec(memory_space=pl.ANY),
        ],
        out_specs=pl.BlockSpec((TB, PEER_U_CHUNKS, LANES), lambda i: (i, 0, 0)),
        out_shape=jax.ShapeDtypeStruct((T, PEER_U_CHUNKS, LANES), jnp.float32),
        scratch_shapes=[
            pltpu.SMEM((PEER_SLOTS * R,), jnp.int32),
            pltpu.VMEM((PEER_SLOTS, PEER_GROUPS_PER_STEP, PEER_CHUNKS_PER_ROW, SUBLANES, LANES), jnp.float32),
            pltpu.SemaphoreType.DMA((PEER_SLOTS,)),
            pltpu.SemaphoreType.DMA((PEER_SLOTS,)),
        ],
        compiler_params=pltpu.CompilerParams(
            dimension_semantics=("arbitrary",), vmem_limit_bytes=VMEM_LIMIT_BYTES),
        name="peer_gather",
    )(e_flat, x2d.reshape(T, PEER_U_CHUNKS, LANES), gain.reshape(PEER_U_CHUNKS, LANES), gt, uv3)
    return out.reshape(T, D)


PEER_ROUTE_TOKENS = 1024
PEER_HALF = PEER_KEY_DIM // 2


def _top16_rows(s, ids=None):
    if ids is None:
        ids = lax.broadcasted_iota(jnp.int32, s.shape, 0)
    vals, idxs = [], []
    for _ in range(PEER_TOPK):
        m = jnp.max(s, axis=0, keepdims=True)
        idx = jnp.min(jnp.where(s == m, ids, jnp.iinfo(jnp.int32).max), axis=0, keepdims=True)
        vals.append(m)
        idxs.append(idx)
        s = jnp.where(ids == idx, -jnp.inf, s)
    return jnp.concatenate(vals, axis=0), jnp.concatenate(idxs, axis=0)


def _pair_candidates(v1, v2):
    n = PEER_TOPK
    tokens = v1.shape[1]
    sums, ids = [], []

    def add(piece, a_of_row, b_of_row):
        ok = (a_of_row + 1) * (b_of_row + 1) <= n
        sums.append(jnp.where(ok, piece, -jnp.inf))
        ids.append(a_of_row * n + b_of_row)

    for a in range(4):
        rows = n if a == 0 else SUBLANES
        b_row = lax.broadcasted_iota(jnp.int32, (rows, tokens), 0)
        add(v1[a:a + 1, :] + v2[0:rows, :], jnp.full_like(b_row, a), b_row)
    for b in range(3):
        rows = n if b == 0 else SUBLANES
        a_row = lax.broadcasted_iota(jnp.int32, (rows, tokens), 0)
        piece = jnp.where(a_row >= 4, v2[b:b + 1, :] + v1[0:rows, :], -jnp.inf)
        add(piece, a_row, jnp.full_like(a_row, b))
    return jnp.concatenate(sums, axis=0), jnp.concatenate(ids, axis=0)


def _pick_rows(table, which):
    out = jnp.zeros_like(table)
    for a in range(PEER_TOPK):
        out = out + jnp.where(which == a, table[a:a + 1, :], 0)
    return out


def _peer_route_kernel(q_ref, keys_ref, e_ref, g_ref):
    qb = q_ref[...].astype(jnp.bfloat16)
    s1 = lax.dot_general(keys_ref[0], qb[:, :PEER_HALF], (((1,), (1,)), ((), ())),
                         preferred_element_type=jnp.float32)
    s2 = lax.dot_general(keys_ref[1], qb[:, PEER_HALF:], (((1,), (1,)), ((), ())),
                         preferred_element_type=jnp.float32)
    v1, i1 = _top16_rows(s1)
    v2, i2 = _top16_rows(s2)
    vals, ci = _top16_rows(*_pair_candidates(v1, v2))
    e_ref[...] = (_pick_rows(i1, lax.shift_right_logical(ci, 4)) * N_KEYS
                  + _pick_rows(i2, ci & (PEER_TOPK - 1)))
    p = jnp.exp(vals - vals[0:1, :])
    g_ref[...] = p / jnp.sum(p, axis=0, keepdims=True)


def peer_route(q_all, sub_keys):
    T = q_all.shape[0]
    TT = PEER_ROUTE_TOKENS
    return pl.pallas_call(
        _peer_route_kernel,
        grid=(T // TT, PEER_HEADS),
        in_specs=[
            pl.BlockSpec((TT, PEER_KEY_DIM), lambda i, h: (i, h)),
            pl.BlockSpec((None, 2, N_KEYS, PEER_HALF), lambda i, h: (h, 0, 0, 0)),
        ],
        out_specs=[
            pl.BlockSpec((PEER_TOPK, TT), lambda i, h: (h, i)),
            pl.BlockSpec((PEER_TOPK, TT), lambda i, h: (h, i)),
        ],
        out_shape=[jax.ShapeDtypeStruct((PEER_ROWS_PER_TOKEN, T), jnp.int32),
                   jax.ShapeDtypeStruct((PEER_ROWS_PER_TOKEN, T), jnp.float32)],
        compiler_params=pltpu.CompilerParams(
            dimension_semantics=("arbitrary", "arbitrary"), vmem_limit_bytes=VMEM_LIMIT_BYTES),
        name="peer_route",
    )(q_all, sub_keys.astype(jnp.bfloat16))


NSA_GROUP = NSA_HEADS // NSA_KV_HEADS
NSA_ROWS = NSA_GROUP * Q_CHUNK
NSA_KEY_TILE = 512


def _qk(qb, k):
    return lax.dot_general(qb, k, (((1,), (1,)), ((), ())), preferred_element_type=jnp.float32)


def _nsa_kernel(q_ref, gl_ref, kc_ref, vc_ref, ov_ref, ks_ref, vs_ref, kw_ref, vw_ref,
                o_ref, mexp_ref):
    c = pl.program_id(1)
    rows = NSA_ROWS
    n_sel = mexp_ref.shape[0] * (NSA_KEY_TILE // SEL_LEN)
    n_cmp_pad = kc_ref.shape[0]
    f32 = jnp.float32
    bf16 = jnp.bfloat16

    qb = (q_ref[...].reshape(rows, HEAD_DIM) * (HEAD_DIM ** -0.5)).astype(bf16)
    t = c * Q_CHUNK + lax.broadcasted_iota(jnp.int32, (Q_CHUNK, 1), 0)

    def bias_rows(allowed):
        return jnp.concatenate([jnp.where(allowed, 0.0, NEG_INF)] * NSA_GROUP, axis=0)

    cmp_end = lax.broadcasted_iota(jnp.int32, (1, n_cmp_pad), 1) * CMP_STRIDE + (CMP_LEN - 1)
    s = _qk(qb, kc_ref[...]) + bias_rows(cmp_end <= t)
    m = jnp.maximum(jnp.max(s, axis=-1, keepdims=True), MASK_FLOOR)
    e = jnp.exp(s - m)
    l = jnp.sum(e, axis=-1, keepdims=True)
    pb = (e / jnp.where(l > 0.0, l, 1.0)).astype(bf16)
    o_cmp = jnp.dot(pb, vc_ref[...], preferred_element_type=f32)

    imp = jnp.dot(pb, ov_ref[...], preferred_element_type=f32)
    imp = jnp.sum(imp.reshape(NSA_GROUP, Q_CHUNK, n_sel), axis=0)
    j = lax.broadcasted_iota(jnp.int32, (Q_CHUNK, n_sel), 1)
    forced = (j == 0) | (j == c) | (j == c - 1)
    imp = jnp.where(forced, SEL_FORCE_SCORE, imp)
    imp = jnp.where(j <= c, imp, -1.0)
    rank = jnp.zeros((Q_CHUNK, n_sel), f32)
    for jp in range(n_sel):
        col = imp[:, jp:jp + 1]
        beats = (col > imp) | ((col == imp) & (j > jp))
        rank = rank + jnp.where(beats, 1.0, 0.0)
    sel = jnp.where((rank < float(SEL_TOP)) & (imp >= 0.0), 1.0, 0.0).astype(bf16)
    blk_of_key = lax.broadcasted_iota(jnp.int32, (n_sel, NSA_KEY_TILE), 1) // SEL_LEN
    blk_row = lax.broadcasted_iota(jnp.int32, (n_sel, NSA_KEY_TILE), 0)
    key_in_tile = lax.broadcasted_iota(jnp.int32, (1, NSA_KEY_TILE), 1)
    for kt in range(mexp_ref.shape[0]):
        expand = jnp.where(blk_of_key + kt * (NSA_KEY_TILE // SEL_LEN) == blk_row, 1.0, 0.0).astype(bf16)
        picked = jnp.dot(sel, expand, preferred_element_type=f32) > 0.5
        mexp_ref[kt] = jnp.where(picked & (key_in_tile + kt * NSA_KEY_TILE <= t), 0.0, NEG_INF)

    def sel_tile(kt, carry):
        m_i, l_i, acc = carry
        off = pl.multiple_of(kt * NSA_KEY_TILE, NSA_KEY_TILE)
        s = (_qk(qb, ks_ref[pl.ds(off, NSA_KEY_TILE), :])
             + jnp.concatenate([mexp_ref[kt]] * NSA_GROUP, axis=0))
        m_new = jnp.maximum(m_i, jnp.max(s, axis=-1, keepdims=True))
        alpha = jnp.exp(m_i - m_new)
        p = jnp.exp(s - m_new)
        l_new = alpha * l_i + jnp.sum(p, axis=-1, keepdims=True)
        acc_new = alpha * acc + jnp.dot(p.astype(bf16), vs_ref[pl.ds(off, NSA_KEY_TILE), :],
                                        preferred_element_type=f32)
        return m_new, l_new, acc_new

    n_tiles = c // (NSA_KEY_TILE // SEL_LEN) + 1
    init = (jnp.full((rows, 1), MASK_FLOOR, f32), jnp.zeros((rows, 1), f32),
            jnp.zeros((rows, HEAD_DIM), f32))
    _, l_s, acc_s = lax.fori_loop(0, n_tiles, sel_tile, init)
    o_sel = acc_s / l_s

    start = pl.multiple_of(jnp.maximum(c * Q_CHUNK - WINDOW, 0), Q_CHUNK)
    cur = pl.multiple_of(c * Q_CHUNK, Q_CHUNK)
    pos_a = start + lax.broadcasted_iota(jnp.int32, (1, WINDOW), 1)
    pos_b = cur + lax.broadcasted_iota(jnp.int32, (1, Q_CHUNK), 1)
    s_a = (_qk(qb, kw_ref[pl.ds(start, WINDOW), :])
           + bias_rows((pos_a < cur) & (pos_a > t - WINDOW)))
    s_b = _qk(qb, kw_ref[pl.ds(cur, Q_CHUNK), :]) + bias_rows(pos_b <= t)
    m_w = jnp.maximum(jnp.max(s_a, axis=-1, keepdims=True), jnp.max(s_b, axis=-1, keepdims=True))
    p_a = jnp.exp(s_a - m_w)
    p_b = jnp.exp(s_b - m_w)
    l_w = jnp.sum(p_a, axis=-1, keepdims=True) + jnp.sum(p_b, axis=-1, keepdims=True)
    o_win = (jnp.dot(p_a.astype(bf16), vw_ref[pl.ds(start, WINDOW), :], preferred_element_type=f32)
             + jnp.dot(p_b.astype(bf16), vw_ref[pl.ds(cur, Q_CHUNK), :], preferred_element_type=f32)) / l_w

    g = jax.nn.sigmoid(gl_ref[...])
    o = g[:, 0:1] * o_cmp + g[:, 1:2] * o_sel + g[:, 2:3] * o_win
    o_ref[...] = o.reshape(NSA_GROUP, Q_CHUNK, HEAD_DIM)


def nsa_attention(q, gl, kc, vc, ks, vs, kw, vw):
    B, S, H, dh = q.shape
    G = NSA_KV_HEADS
    R = NSA_GROUP
    NQ = S // Q_CHUNK
    n_sel = S // SEL_LEN
    n_cmp = kc.shape[2]
    n_cmp_pad = -(-n_cmp // 128) * 128
    bf16 = jnp.bfloat16
    qt = q.reshape(B, S, G, R, dh).transpose(0, 2, 3, 1, 4).reshape(B * G, R, S, dh)
    glt = gl.reshape(B, NQ, Q_CHUNK, G, R, N_BRANCH).transpose(0, 3, 1, 4, 2, 5)
    glt = glt.reshape(B * G, NQ, NSA_ROWS, N_BRANCH)
    pad_c = lambda a: jnp.pad(a.reshape(B * G, n_cmp, dh), ((0, 0), (0, n_cmp_pad - n_cmp), (0, 0))).astype(bf16)
    flat = lambda a: a.reshape(B * G, S, dh).astype(bf16)
    c0 = np.arange(n_cmp_pad)[:, None] * CMP_STRIDE
    s0 = np.arange(n_sel)[None, :] * SEL_LEN
    ov = np.clip(np.minimum(c0 + CMP_LEN, s0 + SEL_LEN) - np.maximum(c0, s0), 0, None) / CMP_LEN
    ov[n_cmp:] = 0.0
    kv_spec = pl.BlockSpec((None, S, dh), lambda bg, c: (bg, 0, 0))
    cmp_spec = pl.BlockSpec((None, n_cmp_pad, dh), lambda bg, c: (bg, 0, 0))
    out = pl.pallas_call(
        _nsa_kernel,
        grid=(B * G, NQ),
        in_specs=[
            pl.BlockSpec((None, R, Q_CHUNK, dh), lambda bg, c: (bg, 0, c, 0)),
            pl.BlockSpec((None, None, NSA_ROWS, N_BRANCH), lambda bg, c: (bg, c, 0, 0)),
            cmp_spec, cmp_spec,
            pl.BlockSpec((n_cmp_pad, n_sel), lambda bg, c: (0, 0)),
            kv_spec, kv_spec, kv_spec, kv_spec,
        ],
        out_specs=pl.BlockSpec((None, R, Q_CHUNK, dh), lambda bg, c: (bg, 0, c, 0)),
        out_shape=jax.ShapeDtypeStruct((B * G, R, S, dh), jnp.float32),
        scratch_shapes=[pltpu.VMEM((S // NSA_KEY_TILE, Q_CHUNK, NSA_KEY_TILE), jnp.float32)],
        compiler_params=pltpu.CompilerParams(
            dimension_semantics=("arbitrary", "arbitrary"), vmem_limit_bytes=VMEM_LIMIT_BYTES),
        name="nsa_attention",
    )(qt, glt, pad_c(kc), pad_c(vc), jnp.asarray(ov, bf16), flat(ks), flat(vs), flat(kw), flat(vw))
    return out.reshape(B, G, R, S, dh).transpose(0, 3, 1, 2, 4).reshape(B, S, H * dh)


def _rmsnorm(x, g):
    y = x * lax.rsqrt(jnp.mean(x * x, axis=-1, keepdims=True) + RMS_EPS)
    return y * g


def _rope(x, pos):
    half = ROPE_DIM // 2
    freqs = ROPE_THETA ** (-jnp.arange(half, dtype=jnp.float32) / half)
    ang = pos.astype(jnp.float32)[:, None] * freqs[None, :]
    cos = jnp.cos(ang)[:, None, :]
    sin = jnp.sin(ang)[:, None, :]
    x1, x2, rest = x[..., :half], x[..., half:ROPE_DIM], x[..., ROPE_DIM:]
    return jnp.concatenate([x1 * cos - x2 * sin, x2 * cos + x1 * sin, rest], axis=-1)


def _nsa_shared_kv(x, positions, kv_norm, w_kv_shared, k_gain_shared, cmp_pos, cmp_w1, cmp_b1, cmp_w2):
    B, S, _ = x.shape
    G, dh = NSA_KV_HEADS, HEAD_DIM
    kv = norm_matmul(x.reshape(B * S, D_MODEL), kv_norm, w_kv_shared).reshape(B, S, N_BRANCH, 2, G, dh)
    n_cmp = (S - CMP_LEN) // CMP_STRIDE + 1
    idx = np.arange(n_cmp)[:, None] * CMP_STRIDE + np.arange(CMP_LEN)[None, :]

    def compress(tok, j):
        blk = tok[:, idx] + cmp_pos[j][None, None, :, None, :]
        blk = blk.transpose(0, 1, 3, 2, 4).reshape(B, n_cmp, G, CMP_LEN * dh)
        hid = jax.nn.gelu(blk @ cmp_w1[j] + cmp_b1[j])
        return hid @ cmp_w2[j]

    kc = _rope(_rmsnorm(compress(kv[:, :, 0, 0], 0), k_gain_shared[0]), positions[idx[:, -1]])
    vc = compress(kv[:, :, 0, 1], 1)
    ks = _rope(_rmsnorm(kv[:, :, 1, 0], k_gain_shared[1]), positions)
    vs = kv[:, :, 1, 1]
    kw = _rope(_rmsnorm(kv[:, :, 2, 0], k_gain_shared[2]), positions)
    vw = kv[:, :, 2, 1]
    t = lambda a: a.transpose(0, 2, 1, 3)
    return (t(kc), t(vc), t(ks), t(vs), t(kw), t(vw))


def _peer_residual(x2d, norm_g, w_q, sub_keys, u_tab, v_tab):
    q_all = norm_matmul(x2d, norm_g, w_q)
    e_t, g_t = peer_route(q_all, sub_keys)
    uv = jnp.concatenate([u_tab, v_tab], axis=1)
    return peer_gather(x2d, norm_g, e_t.T, g_t.T, uv)


def kernel(x, mem, positions, norm_mix, norm_ffn, norm_mem, w_out, w_mem_kv, mem_q_gain, mem_k_gain, a_w_in, a_conv_w, a_conv_b, a_gate_w, a_gate_b, a_lambda, b_w_in, b_gate_b, b_q_gain, kv_norm, w_kv_shared, k_gain_shared, cmp_pos, cmp_w1, cmp_b1, cmp_w2, peer_wq, peer_subkeys, peer_u, peer_v):
    B, S, D = x.shape
    T = B * S
    x2d = x.reshape(T, D)
    shared = None
    for l in range(DEPTH):
        if l < N_A:
            proj = norm_matmul(x2d, norm_mix[l], a_w_in[l])
            qm = proj[:, 2 * LRU_WIDTH:]
            mix = rg_lru_mix(proj, a_conv_w[l], a_conv_b[l], a_gate_w[l], a_gate_b[l], a_lambda[l], B)
        else:
            j = l - N_A
            proj = norm_matmul(x2d, norm_mix[l], b_w_in[j])
            qm = proj[:, NSA_WIDTH + N_BRANCH * NSA_HEADS:]
            proj = proj.reshape(B, S, -1)
            q = proj[..., :NSA_WIDTH].reshape(B, S, NSA_HEADS, HEAD_DIM)
            gl = (proj[..., NSA_WIDTH:NSA_WIDTH + N_BRANCH * NSA_HEADS].reshape(B, S, NSA_HEADS, N_BRANCH)
                  + b_gate_b[j].reshape(NSA_HEADS, N_BRANCH))
            q = _rope(_rmsnorm(q, b_q_gain[j]), positions)
            mix = nsa_attention(q, gl, *shared).reshape(T, NSA_WIDTH)
        kv_mem = norm_matmul(mem.reshape(-1, D), norm_mem[l], w_mem_kv[l])
        x2d = mem_out(mix, qm, kv_mem, mem_q_gain[l], mem_k_gain[l], w_out[l], x2d, B)
        x2d = _peer_residual(x2d, norm_ffn[l], peer_wq[l], peer_subkeys[l], peer_u[l], peer_v[l])
        if l == N_A - 1:
            shared = _nsa_shared_kv(x2d.reshape(B, S, D), positions, kv_norm, w_kv_shared, k_gain_shared,
                                    cmp_pos, cmp_w1, cmp_b1, cmp_w2)
    return x2d.reshape(B, S, D)
```
